```python
import math
import jax
import jax.numpy as jnp
from jax import lax
import numpy as np

D_MODEL = 1024
BATCH = 8
SEQ = 4096
DEPTH = 1

HEAD_DIM = 64
N_HEADS_NA = 8
N_HEADS_DIL = 8
N_HEADS = N_HEADS_NA + N_HEADS_DIL
D_NA = N_HEADS_NA * HEAD_DIM
D_DIL = N_HEADS_DIL * HEAD_DIM
D_MIX = D_NA + D_DIL
D_FF = 2816
GRID_W = 64
NA_KH = 8
NA_KW = 16
NA_COL_BLOCK = 16
NA_COL_SPAN = 32
DIL_PATTERNS = ((128, 1), (512, 4), (2048, 16))
DIL_QBLOCK = 128
T5_BUCKETS = 32
T5_MAX_DIST = 1024
NORM_EPS = 1e-6
MASK_VALUE = -1e30

kernel_name = 'hybrid_na_dilated_macaron_layer'


def rmsnorm(x, g):
    xf = x.astype(jnp.float32)
    y = xf * lax.rsqrt(jnp.mean(xf * xf, axis=-1, keepdims=True) + NORM_EPS)
    return (y * g.astype(jnp.float32)).astype(x.dtype)


def swiglu(x, w_gate, w_up, w_down):
    return (jax.nn.silu(x @ w_gate) * (x @ w_up)) @ w_down


def t5_bucket(rel):
    nb = T5_BUCKETS // 2
    max_exact = nb // 2
    n = jnp.abs(rel)
    large = max_exact + (jnp.log(jnp.maximum(n, 1).astype(jnp.float32) / max_exact)
                         / math.log(T5_MAX_DIST / max_exact) * (nb - max_exact)).astype(jnp.int32)
    large = jnp.minimum(large, nb - 1)
    return jnp.where(rel > 0, nb, 0) + jnp.where(n < max_exact, n, large)


def neighbourhood_attention(q, k, v, rel_bias):
    B, H, S, Dh = q.shape
    rows = S // GRID_W
    kh = min(NA_KH, rows)
    ncb = GRID_W // NA_COL_BLOCK
    n_keys = kh * NA_COL_SPAN
    r = jnp.arange(rows)
    key_rows = jnp.clip(r - kh // 2, 0, rows - kh)[:, None] + jnp.arange(kh)[None, :]
    cb = jnp.arange(ncb)
    key_cols = (jnp.clip(cb * NA_COL_BLOCK - NA_KW // 2, 0, GRID_W - NA_COL_SPAN)[:, None]
                + jnp.arange(NA_COL_SPAN)[None, :])
    key_idx = (key_rows[:, None, :, None] * GRID_W
               + key_cols[None, :, None, :]).reshape(rows, ncb, n_keys)
    kr = jnp.repeat(key_rows, NA_COL_SPAN, axis=1)
    kc = jnp.tile(key_cols, (1, kh))
    q_col = cb[:, None] * NA_COL_BLOCK + jnp.arange(NA_COL_BLOCK)[None, :]
    q_col_start = jnp.clip(q_col - NA_KW // 2, 0, GRID_W - NA_KW)
    col_valid = ((kc[:, None, :] >= q_col_start[:, :, None])
                 & (kc[:, None, :] < q_col_start[:, :, None] + NA_KW))
    dc_idx = jnp.clip(kc[:, None, :] - q_col[:, :, None] + NA_KW - 1, 0, 2 * NA_KW - 2)
    dr_idx = kr - r[:, None] + NA_KH - 1
    scale = HEAD_DIM ** -0.5
    q_rows = jnp.moveaxis(q.reshape(B, H, rows, ncb, NA_COL_BLOCK, Dh), 2, 0)

    def row_block(args):
        q_r, idx_r, dr_r = args
        k_r = k[:, :, idx_r]
        v_r = v[:, :, idx_r]
        s = jnp.einsum('bhnqe,bhnke->bhnqk', q_r, k_r,
                       preferred_element_type=jnp.float32) * scale
        bias = rel_bias[:, dr_r[None, None, :], dc_idx].astype(jnp.float32)
        s = jnp.where(col_valid, s + bias, MASK_VALUE)
        p = jax.nn.softmax(s, axis=-1)
        return jnp.einsum('bhnqk,bhnke->bhnqe', p.astype(v.dtype), v_r)

    out = lax.map(row_block, (q_rows, key_idx, dr_idx))
    return jnp.moveaxis(out, 0, 2).reshape(B, H, S, Dh)


def dilated_window_attention(q, k, v, t5_bias, window, dilation):
    B, H, S, Dh = q.shape
    radius = window // (2 * dilation)
    L = S // dilation
    qb = min(DIL_QBLOCK, L)
    nb = -(-L // qb)
    Lp = nb * qb
    kb_len = qb + 2 * radius

    def to_sub(t):
        return t.reshape(B, H, L, dilation, Dh).transpose(0, 1, 3, 2, 4)

    qs = jnp.pad(to_sub(q), ((0, 0), (0, 0), (0, 0), (0, Lp - L), (0, 0))).reshape(B, H, dilation, nb, qb, Dh)
    pad_k = ((0, 0), (0, 0), (0, 0), (radius, radius + Lp - L), (0, 0))
    ks = jnp.pad(to_sub(k), pad_k)
    vs = jnp.pad(to_sub(v), pad_k)
    k_idx = jnp.arange(nb)[:, None] * qb + jnp.arange(kb_len)[None, :]
    k_blk = ks[:, :, :, k_idx]
    v_blk = vs[:, :, :, k_idx]
    q_pos = jnp.arange(nb)[:, None] * qb + jnp.arange(qb)[None, :]
    k_pos = k_idx - radius
    off = k_pos[:, None, :] - q_pos[:, :, None]
    valid = (jnp.abs(off) <= radius) & (k_pos[:, None, :] >= 0) & (k_pos[:, None, :] < L)
    bias = t5_bias[:, t5_bucket(off * dilation)].astype(jnp.float32)
    s = jnp.einsum('bhdnqe,bhdnke->bhdnqk', qs, k_blk,
                   preferred_element_type=jnp.float32) * HEAD_DIM ** -0.5
    s = jnp.where(valid, s + bias[:, None], MASK_VALUE)
    m = jnp.max(s, axis=-1, keepdims=True)
    p = jnp.exp(s - m)
    den = jnp.sum(p, axis=-1)
    o = jnp.einsum('bhdnqk,bhdnke->bhdnqe', p.astype(v.dtype), v_blk,
                   preferred_element_type=jnp.float32) / den[..., None]
    lse = m[..., 0] + jnp.log(den)
    o = o.reshape(B, H, dilation, Lp, Dh)[:, :, :, :L].transpose(0, 1, 3, 2, 4).reshape(B, H, S, Dh)
    lse = lse.reshape(B, H, dilation, Lp)[..., :L].transpose(0, 1, 3, 2).reshape(B, H, S)
    return o, lse


def dilated_mixture_attention(q, k, v, t5_bias):
    results = [dilated_window_attention(q, k, v, t5_bias, w, d) for w, d in DIL_PATTERNS]
    outs = jnp.stack([res[0] for res in results])
    lses = jnp.stack([res[1] for res in results])
    wts = jax.nn.softmax(lses, axis=0)
    return jnp.sum(wts[..., None] * outs, axis=0).astype(q.dtype)


def setup_inputs(seed: int = 0) -> dict:
    key = jax.random.key(seed)
    ks = jax.random.split(key, 20)

    def nrm(k, shape, scale):
        return jax.random.normal(k, shape, jnp.float32) * scale

    def gain(k, shape):
        return 1.0 + 0.05 * jax.random.normal(k, shape, jnp.float32)

    return {
        'x': nrm(ks[0], (BATCH, SEQ, D_MODEL), 1.0),
        'ffn1_pre_g': gain(ks[1], (DEPTH, D_MODEL)),
        'ffn1_w_gate': nrm(ks[2], (DEPTH, D_MODEL, D_FF), D_MODEL ** -0.5),
        'ffn1_w_up': nrm(ks[3], (DEPTH, D_MODEL, D_FF), D_MODEL ** -0.5),
        'ffn1_w_down': nrm(ks[4], (DEPTH, D_FF, D_MODEL), D_FF ** -0.5),
        'ffn1_post_g': gain(ks[5], (DEPTH, D_MODEL)),
        'mix_pre_g': gain(ks[6], (DEPTH, D_MODEL)),
        'w_qkv': nrm(ks[7], (DEPTH, D_MODEL, 3 * D_MIX), D_MODEL ** -0.5),
        'na_rel_bias': nrm(ks[8], (DEPTH, N_HEADS_NA, 2 * NA_KH - 1, 2 * NA_KW - 1), 0.5),
        't5_rel_bias': nrm(ks[9], (N_HEADS_DIL, T5_BUCKETS), 0.5),
        'na_out_g': gain(ks[10], (DEPTH, D_NA)),
        'dil_out_g': gain(ks[11], (DEPTH, D_DIL)),
        'w_out': nrm(ks[12], (DEPTH, D_MIX, D_MODEL), D_MIX ** -0.5),
        'mix_post_g': gain(ks[13], (DEPTH, D_MODEL)),
        'ffn2_pre_g': gain(ks[14], (DEPTH, D_MODEL)),
        'ffn2_w_gate': nrm(ks[15], (DEPTH, D_MODEL, D_FF), D_MODEL ** -0.5),
        'ffn2_w_up': nrm(ks[16], (DEPTH, D_MODEL, D_FF), D_MODEL ** -0.5),
        'ffn2_w_down': nrm(ks[17], (DEPTH, D_FF, D_MODEL), D_FF ** -0.5),
        'ffn2_post_g': gain(ks[18], (DEPTH, D_MODEL)),
    }


def reference(x, ffn1_pre_g, ffn1_w_gate, ffn1_w_up, ffn1_w_down, ffn1_post_g,
              mix_pre_g, w_qkv, na_rel_bias, t5_rel_bias, na_out_g, dil_out_g, w_out,
              mix_post_g, ffn2_pre_g, ffn2_w_gate, ffn2_w_up, ffn2_w_down, ffn2_post_g):
    B, S, _ = x.shape
    for l in range(DEPTH):
        h = swiglu(rmsnorm(x, ffn1_pre_g[l]), ffn1_w_gate[l], ffn1_w_up[l], ffn1_w_down[l])
        x = x + 0.5 * rmsnorm(h, ffn1_post_g[l])
        h = rmsnorm(x, mix_pre_g[l])
        qkv = (h @ w_qkv[l]).reshape(B, S, 3, N_HEADS, HEAD_DIM).transpose(2, 0, 3, 1, 4)
        q, k, v = qkv[0], qkv[1], qkv[2]
        o_na = neighbourhood_attention(q[:, :N_HEADS_NA], k[:, :N_HEADS_NA], v[:, :N_HEADS_NA],
                                       na_rel_bias[l])
        o_dil = dilated_mixture_attention(q[:, N_HEADS_NA:], k[:, N_HEADS_NA:], v[:, N_HEADS_NA:],
                                          t5_rel_bias)
        o_na = rmsnorm(o_na.transpose(0, 2, 1, 3).reshape(B, S, D_NA), na_out_g[l])
        o_dil = rmsnorm(o_dil.transpose(0, 2, 1, 3).reshape(B, S, D_DIL), dil_out_g[l])
        mixed = jnp.concatenate([o_na, o_dil], axis=-1) @ w_out[l]
        x = x + rmsnorm(mixed, mix_post_g[l])
        h = swiglu(rmsnorm(x, ffn2_pre_g[l]), ffn2_w_gate[l], ffn2_w_up[l], ffn2_w_down[l])
        x = x + 0.5 * rmsnorm(h, ffn2_post_g[l])
    return x
```

```python
import functools
import math

import jax
import jax.numpy as jnp
import numpy as np
from jax import lax
from jax.experimental import pallas as pl
from jax.experimental.pallas import tpu as pltpu

D_MODEL = 1024
HEAD_DIM = 64
N_HEADS_NA = 8
N_HEADS_DIL = 8
D_NA = N_HEADS_NA * HEAD_DIM
D_DIL = N_HEADS_DIL * HEAD_DIM
D_FF = 2816
GRID_W = 64
NA_KH = 8
NA_KW = 16
DIL_PATTERNS = ((128, 1), (512, 4), (2048, 16))
DIL_QBLOCK = 128
DIL_RADIUS = 64
T5_BUCKETS = 32
T5_MAX_DIST = 1024
NORM_EPS = 1e-6

LANES = 128
HEADS_PER_TILE = LANES // HEAD_DIM
N_PAIRS = N_HEADS_NA // HEADS_PER_TILE
D_GROUP = 3 * D_NA
VMEM_LIMIT = 56 * 1024 * 1024

BF16 = jnp.bfloat16
F32 = jnp.float32


def _rms(x, g):
    return x * lax.rsqrt(jnp.mean(x * x, axis=-1, keepdims=True) + NORM_EPS) * g


def _resident(shape):
    return pl.BlockSpec(shape, lambda *_: (0,) * len(shape), pipeline_mode=pl.Buffered(1))


def _ffn_kernel(x_ref, pre_g_ref, wg_ref, wu_ref, wd_ref, post_g_ref, o_ref, *, n_chunks):
    x = x_ref[...]
    h = _rms(x, pre_g_ref[...]).astype(BF16)
    ck = D_FF // n_chunks
    y = None
    for c in range(n_chunks):
        g = jnp.dot(h, wg_ref[:, c * ck:(c + 1) * ck], preferred_element_type=F32)
        u = jnp.dot(h, wu_ref[:, c * ck:(c + 1) * ck], preferred_element_type=F32)
        a = (g * jax.nn.sigmoid(g) * u).astype(BF16)
        part = jnp.dot(a, wd_ref[c * ck:(c + 1) * ck, :], preferred_element_type=F32)
        y = part if y is None else y + part
    o_ref[...] = x + 0.5 * _rms(y, post_g_ref[...])


def _ffn(x2d, pre_g, wg, wu, wd, post_g, *, tm=256, n_chunks=2):
    n = x2d.shape[0]
    return pl.pallas_call(
        functools.partial(_ffn_kernel, n_chunks=n_chunks),
        grid=(n // tm,),
        in_specs=[
            pl.BlockSpec((tm, D_MODEL), lambda i: (i, 0)),
            _resident((1, D_MODEL)),
            _resident((D_MODEL, D_FF)),
            _resident((D_MODEL, D_FF)),
            _resident((D_FF, D_MODEL)),
            _resident((1, D_MODEL)),
        ],
        out_specs=pl.BlockSpec((tm, D_MODEL), lambda i: (i, 0)),
        out_shape=jax.ShapeDtypeStruct((n, D_MODEL), F32),
        compiler_params=pltpu.CompilerParams(
            dimension_semantics=("arbitrary",), vmem_limit_bytes=VMEM_LIMIT),
        name="ffn",
    )(x2d, pre_g, wg, wu, wd, post_g)


def _qkv_kernel(x_ref, g_ref, w_ref, na_ref, d1_ref, d4_ref, d16_ref, scr_ref, *, tm):
    h = _rms(x_ref[0], g_ref[...]).astype(BF16)
    y = jnp.dot(h, w_ref[...], preferred_element_type=F32)
    na_ref[0] = y[:, :D_GROUP].astype(BF16)
    yd = y[:, D_GROUP:]
    d1_ref[0] = yd.astype(BF16)
    n_slabs = D_GROUP // LANES
    for s in range(n_slabs):
        scr_ref[s] = yd[:, s * LANES:(s + 1) * LANES]
    for s in range(n_slabs):
        for r in range(4):
            d4_ref[0, r, :, s * LANES:(s + 1) * LANES] = (
                scr_ref[s, pl.ds(r, tm // 4, stride=4), :].astype(BF16))
        for r in range(16):
            d16_ref[0, r, :, s * LANES:(s + 1) * LANES] = (
                scr_ref[s, pl.ds(r, tm // 16, stride=16), :].astype(BF16))


def _qkv(x, g, w, *, tm=256):
    b, s, _ = x.shape
    return pl.pallas_call(
        functools.partial(_qkv_kernel, tm=tm),
        grid=(b, s // tm),
        in_specs=[
            pl.BlockSpec((1, tm, D_MODEL), lambda bi, i: (bi, i, 0)),
            _resident((1, D_MODEL)),
            _resident((D_MODEL, 2 * D_GROUP)),
        ],
        out_specs=[
            pl.BlockSpec((1, tm, D_GROUP), lambda bi, i: (bi, i, 0)),
            pl.BlockSpec((1, tm, D_GROUP), lambda bi, i: (bi, i, 0)),
            pl.BlockSpec((1, 4, tm // 4, D_GROUP), lambda bi, i: (bi, 0, i, 0)),
            pl.BlockSpec((1, 16, tm // 16, D_GROUP), lambda bi, i: (bi, 0, i, 0)),
        ],
        out_shape=[
            jax.ShapeDtypeStruct((b, s, D_GROUP), BF16),
            jax.ShapeDtypeStruct((b, s, D_GROUP), BF16),
            jax.ShapeDtypeStruct((b, 4, s // 4, D_GROUP), BF16),
            jax.ShapeDtypeStruct((b, 16, s // 16, D_GROUP), BF16),
        ],
        scratch_shapes=[pltpu.VMEM((D_GROUP // LANES, tm, LANES), F32)],
        compiler_params=pltpu.CompilerParams(
            dimension_semantics=("arbitrary", "arbitrary"), vmem_limit_bytes=VMEM_LIMIT),
        name="qkv",
    )(x, g, w)


def _head_masks():
    lane = lax.broadcasted_iota(jnp.int32, (1, LANES), 1)
    return [(lane // HEAD_DIM) == h for h in range(HEADS_PER_TILE)]


def _block_attention(q, k, v, tables):
    masks = _head_masks()
    m_t = l_t = acc_t = None
    for h in range(HEADS_PER_TILE):
        qh = jnp.where(masks[h], q, jnp.zeros_like(q))
        s = lax.dot_general(qh, k, (((1,), (1,)), ((), ())), preferred_element_type=F32)
        s = s + tables[h]
        m = jnp.max(s, axis=-1, keepdims=True)
        p = jnp.exp(s - m)
        l = jnp.sum(p, axis=-1, keepdims=True)
        pv = jnp.dot(p.astype(BF16), v, preferred_element_type=F32)
        if h == 0:
            m_t = jnp.broadcast_to(m, pv.shape)
            l_t = jnp.broadcast_to(l, pv.shape)
            acc_t = pv
        else:
            m_t = jnp.where(masks[h], m, m_t)
            l_t = jnp.where(masks[h], l, l_t)
            acc_t = jnp.where(masks[h], pv, acc_t)
    return m_t, l_t, acc_t


def _na_kernel(q_ref, k_ref, v_ref, tbl_ref, o_ref):
    rows = q_ref.shape[1] // GRID_W
    n_keys = NA_KH * GRID_W

    def row(r, carry):
        kr0 = jnp.clip(r - NA_KH // 2, 0, rows - NA_KH)
        var = r - kr0
        q0 = pl.multiple_of(r * GRID_W, GRID_W)
        k0 = pl.multiple_of(kr0 * GRID_W, GRID_W)
        q = q_ref[0, pl.ds(q0, GRID_W), :]
        k = k_ref[0, pl.ds(k0, n_keys), :]
        v = v_ref[0, pl.ds(k0, n_keys), :]
        tables = [tbl_ref[0, h, var] for h in range(HEADS_PER_TILE)]
        _, l_t, acc_t = _block_attention(q, k, v, tables)
        o_ref[0, pl.ds(q0, GRID_W), :] = acc_t / l_t
        return carry

    lax.fori_loop(0, rows, row, 0)


def _na(qkv_na, tbl):
    b, s, _ = qkv_na.shape
    blk = lambda off: pl.BlockSpec((1, s, LANES), lambda bi, p: (bi, 0, off + p))
    return pl.pallas_call(
        _na_kernel,
        grid=(b, N_PAIRS),
        in_specs=[
            blk(0), blk(N_PAIRS), blk(2 * N_PAIRS),
            pl.BlockSpec((1,) + tbl.shape[1:], lambda bi, p: (p, 0, 0, 0, 0)),
        ],
        out_specs=pl.BlockSpec((1, s, LANES), lambda bi, p: (bi, 0, p)),
        out_shape=jax.ShapeDtypeStruct((b, s, D_NA), F32),
        compiler_params=pltpu.CompilerParams(
            dimension_semantics=("arbitrary", "arbitrary"), vmem_limit_bytes=VMEM_LIMIT),
        name="na_attn",
    )(qkv_na, qkv_na, qkv_na, tbl)


def _dil_kernel(q1_ref, k1_ref, v1_ref, q4_ref, k4_ref, v4_ref, q16_ref, k16_ref, v16_ref,
                tbl_ref, o_ref, m_ref, l_ref, acc_ref):
    s_len = q1_ref.shape[1]
    qb = DIL_QBLOCK
    kb = qb + 2 * DIL_RADIUS

    def window(i, n_blocks):
        q0 = pl.multiple_of(i * qb, qb)
        w0 = pl.multiple_of(jnp.clip(q0 - DIL_RADIUS, 0, n_blocks * qb - kb), DIL_RADIUS)
        var = jnp.where(i == 0, 0, jnp.where(i == n_blocks - 1, 2, 1))
        return q0, w0, var

    nb1 = s_len // qb

    def p1(i, carry):
        q0, w0, var = window(i, nb1)
        tables = [tbl_ref[0, h, 0, var] for h in range(HEADS_PER_TILE)]
        m_t, l_t, acc_t = _block_attention(
            q1_ref[0, pl.ds(q0, qb), :], k1_ref[0, pl.ds(w0, kb), :],
            v1_ref[0, pl.ds(w0, kb), :], tables)
        m_ref[pl.ds(q0, qb), :] = m_t
        l_ref[pl.ds(q0, qb), :] = l_t
        acc_ref[pl.ds(q0, qb), :] = acc_t
        return carry

    lax.fori_loop(0, nb1, p1, 0)

    def strided_pattern(pat, dil, q_ref, k_ref, v_ref):
        nb = s_len // dil // qb

        def body(j, carry):
            r = j // nb
            i = j % nb
            q0, w0, var = window(i, nb)
            tables = [tbl_ref[0, h, pat, var] for h in range(HEADS_PER_TILE)]
            m_b, l_b, acc_b = _block_attention(
                q_ref[0, r, pl.ds(q0, qb), :], k_ref[0, r, pl.ds(w0, kb), :],
                v_ref[0, r, pl.ds(w0, kb), :], tables)
            rows = pl.ds(r + q0 * dil, qb, stride=dil)
            m_old = m_ref[rows, :]
            m_new = jnp.maximum(m_old, m_b)
            a = jnp.exp(m_old - m_new)
            bb = jnp.exp(m_b - m_new)
            m_ref[rows, :] = m_new
            l_ref[rows, :] = a * l_ref[rows, :] + bb * l_b
            acc_ref[rows, :] = a * acc_ref[rows, :] + bb * acc_b
            return carry

        lax.fori_loop(0, dil * nb, body, 0)

    strided_pattern(1, 4, q4_ref, k4_ref, v4_ref)
    strided_pattern(2, 16, q16_ref, k16_ref, v16_ref)

    def fin(i, carry):
        q0 = pl.multiple_of(i * qb, qb)
        o_ref[0, pl.ds(q0, qb), :] = acc_ref[pl.ds(q0, qb), :] / l_ref[pl.ds(q0, qb), :]
        return carry

    lax.fori_loop(0, nb1, fin, 0)


def _dil(d1, d4, d16, tbl):
    b, s, _ = d1.shape
    blk1 = lambda off: pl.BlockSpec((1, s, LANES), lambda bi, p: (bi, 0, off + p))
    blkd = lambda d, off: pl.BlockSpec((1, d, s // d, LANES), lambda bi, p: (bi, 0, 0, off + p))
    specs = [blk1(0), blk1(N_PAIRS), blk1(2 * N_PAIRS)]
    for d in (4, 16):
        specs += [blkd(d, 0), blkd(d, N_PAIRS), blkd(d, 2 * N_PAIRS)]
    specs.append(pl.BlockSpec((1,) + tbl.shape[1:], lambda bi, p: (p, 0, 0, 0, 0, 0)))
    return pl.pallas_call(
        _dil_kernel,
        grid=(b, N_PAIRS),
        in_specs=specs,
        out_specs=pl.BlockSpec((1, s, LANES), lambda bi, p: (bi, 0, p)),
        out_shape=jax.ShapeDtypeStruct((b, s, D_DIL), F32),
        scratch_shapes=[pltpu.VMEM((s, LANES), F32)] * 3,
        compiler_params=pltpu.CompilerParams(
            dimension_semantics=("arbitrary", "arbitrary"), vmem_limit_bytes=VMEM_LIMIT),
        name="dil_attn",
    )(d1, d1, d1, d4, d4, d4, d16, d16, d16, tbl)


def _out_kernel(x_ref, na_ref, dil_ref, g_na_ref, g_dil_ref, w_ref, g_post_ref, o_ref):
    a = _rms(na_ref[...], g_na_ref[...]).astype(BF16)
    d = _rms(dil_ref[...], g_dil_ref[...]).astype(BF16)
    mixed = (jnp.dot(a, w_ref[:D_NA, :], preferred_element_type=F32)
             + jnp.dot(d, w_ref[D_NA:, :], preferred_element_type=F32))
    o_ref[...] = x_ref[...] + _rms(mixed, g_post_ref[...])


def _outproj(x2d, o_na, o_dil, g_na, g_dil, w, g_post, *, tm=512):
    n = x2d.shape[0]
    row = lambda width: pl.BlockSpec((tm, width), lambda i: (i, 0))
    return pl.pallas_call(
        _out_kernel,
        grid=(n // tm,),
        in_specs=[row(D_MODEL), row(D_NA), row(D_DIL), _resident((1, D_NA)),
                  _resident((1, D_DIL)), _resident((D_NA + D_DIL, D_MODEL)),
                  _resident((1, D_MODEL))],
        out_specs=row(D_MODEL),
        out_shape=jax.ShapeDtypeStruct((n, D_MODEL), F32),
        compiler_params=pltpu.CompilerParams(
            dimension_semantics=("arbitrary",), vmem_limit_bytes=VMEM_LIMIT),
        name="outproj",
    )(x2d, o_na, o_dil, g_na, g_dil, w, g_post)


def _t5_bucket(rel):
    nb = T5_BUCKETS // 2
    max_exact = nb // 2
    n = jnp.abs(rel)
    large = max_exact + (jnp.log(jnp.maximum(n, 1).astype(F32) / max_exact)
                         / math.log(T5_MAX_DIST / max_exact) * (nb - max_exact)).astype(jnp.int32)
    large = jnp.minimum(large, nb - 1)
    return jnp.where(rel > 0, nb, 0) + jnp.where(n < max_exact, n, large)


def _na_tables(rel_bias):
    var = np.arange(NA_KH)[:, None, None, None]
    qc = np.arange(GRID_W)[None, :, None, None]
    ki = np.arange(NA_KH)[None, None, :, None]
    kc = np.arange(GRID_W)[None, None, None, :]
    dr = np.broadcast_to(ki - var + NA_KH - 1, (NA_KH, GRID_W, NA_KH, GRID_W))
    dc = np.broadcast_to(np.clip(kc - qc + NA_KW - 1, 0, 2 * NA_KW - 2), dr.shape)
    qs = np.clip(qc - NA_KW // 2, 0, GRID_W - NA_KW)
    valid = np.broadcast_to((kc >= qs) & (kc < qs + NA_KW), dr.shape)
    bias = rel_bias.astype(F32)[:, dr, dc]
    tbl = jnp.where(valid[None], bias, -jnp.inf)
    return tbl.reshape(N_PAIRS, HEADS_PER_TILE, NA_KH, GRID_W, NA_KH * GRID_W)


def _dil_tables(t5_bias):
    qq = np.arange(DIL_QBLOCK)[None, :, None]
    kk = np.arange(DIL_QBLOCK + 2 * DIL_RADIUS)[None, None, :]
    var = np.arange(3)[:, None, None]
    off = kk - DIL_RADIUS * var - qq
    valid = np.abs(off) <= DIL_RADIUS
    per_pattern = []
    for _, dil in DIL_PATTERNS:
        bucket = _t5_bucket(jnp.asarray(off * dil, jnp.int32))
        bias = t5_bias.astype(F32)[:, bucket]
        per_pattern.append(jnp.where(valid[None], bias, -jnp.inf))
    tbl = jnp.stack(per_pattern, axis=1)
    return tbl.reshape((N_PAIRS, HEADS_PER_TILE) + tbl.shape[1:])


def _group_columns(w_qkv):
    w = w_qkv.reshape(D_MODEL, 3, N_HEADS_NA + N_HEADS_DIL, HEAD_DIM)
    w = w * jnp.asarray([HEAD_DIM ** -0.5, 1.0, 1.0], F32)[None, :, None, None]
    na = w[:, :, :N_HEADS_NA].reshape(D_MODEL, D_GROUP)
    dil = w[:, :, N_HEADS_NA:].reshape(D_MODEL, D_GROUP)
    return jnp.concatenate([na, dil], axis=1).astype(BF16)


def kernel(x, ffn1_pre_g, ffn1_w_gate, ffn1_w_up, ffn1_w_down, ffn1_post_g, mix_pre_g, w_qkv,
           na_rel_bias, t5_rel_bias, na_out_g, dil_out_g, w_out, mix_post_g, ffn2_pre_g,
           ffn2_w_gate, ffn2_w_up, ffn2_w_down, ffn2_post_g):
    b, s, d = x.shape
    depth = ffn1_pre_g.shape[0]
    dil_tbl = _dil_tables(t5_rel_bias)
    for l in range(depth):
        x2d = _ffn(x.reshape(b * s, d), ffn1_pre_g[l][None], ffn1_w_gate[l].astype(BF16),
                   ffn1_w_up[l].astype(BF16), ffn1_w_down[l].astype(BF16), ffn1_post_g[l][None])
        qkv_na, d1, d4, d16 = _qkv(x2d.reshape(b, s, d), mix_pre_g[l][None],
                                   _group_columns(w_qkv[l]))
        o_na = _na(qkv_na, _na_tables(na_rel_bias[l]))
        o_dil = _dil(d1, d4, d16, dil_tbl)
        x2d = _outproj(x2d, o_na.reshape(b * s, D_NA), o_dil.reshape(b * s, D_DIL),
                       na_out_g[l][None], dil_out_g[l][None], w_out[l].astype(BF16),
                       mix_post_g[l][None])
        x2d = _ffn(x2d, ffn2_pre_g[l][None], ffn2_w_gate[l].astype(BF16),
                   ffn2_w_up[l].astype(BF16), ffn2_w_down[l].astype(BF16), ffn2_post_g[l][None])
        x = x2d.reshape(b, s, d)
    return x
```

```python
import functools
import math

import jax
import jax.numpy as jnp
import numpy as np
from jax import lax
from jax.experimental import pallas as pl
from jax.experimental.pallas import tpu as pltpu

D_MODEL = 1024
HEAD_DIM = 64
N_HEADS_NA = 8
N_HEADS_DIL = 8
D_NA = N_HEADS_NA * HEAD_DIM
D_DIL = N_HEADS_DIL * HEAD_DIM
D_FF = 2816
GRID_W = 64
NA_KH = 8
NA_KW = 16
DIL_PATTERNS = ((128, 1), (512, 4), (2048, 16))
DIL_QBLOCK = 128
DIL_RADIUS = 64
DIL_KBLOCK = DIL_QBLOCK + 2 * DIL_RADIUS
T5_BUCKETS = 32
T5_MAX_DIST = 1024
NORM_EPS = 1e-6

LANES = 128
HEADS_PER_TILE = LANES // HEAD_DIM
N_PAIRS = N_HEADS_NA // HEADS_PER_TILE
D_GROUP = 3 * D_NA
VMEM_LIMIT = 56 * 1024 * 1024

BF16 = jnp.bfloat16
F32 = jnp.float32


def _rms(x, g):
    return x * lax.rsqrt(jnp.mean(x * x, axis=-1, keepdims=True) + NORM_EPS) * g


def _resident(shape):
    return pl.BlockSpec(shape, lambda *_: (0,) * len(shape), pipeline_mode=pl.Buffered(1))


def _ffn_kernel(x_ref, pre_g_ref, wg_ref, wu_ref, wd_ref, post_g_ref, o_ref, *, n_chunks):
    x = x_ref[...]
    h = _rms(x, pre_g_ref[...]).astype(BF16)
    ck = D_FF // n_chunks
    y = None
    for c in range(n_chunks):
        g = jnp.dot(h, wg_ref[:, c * ck:(c + 1) * ck], preferred_element_type=F32)
        u = jnp.dot(h, wu_ref[:, c * ck:(c + 1) * ck], preferred_element_type=F32)
        a = (g * jax.nn.sigmoid(g) * u).astype(BF16)
        part = jnp.dot(a, wd_ref[c * ck:(c + 1) * ck, :], preferred_element_type=F32)
        y = part if y is None else y + part
    o_ref[...] = x + 0.5 * _rms(y, post_g_ref[...])


def _ffn(x2d, pre_g, wg, wu, wd, post_g, *, tm=256, n_chunks=2):
    n = x2d.shape[0]
    return pl.pallas_call(
        functools.partial(_ffn_kernel, n_chunks=n_chunks),
        grid=(n // tm,),
        in_specs=[
            pl.BlockSpec((tm, D_MODEL), lambda i: (i, 0)),
            _resident((1, D_MODEL)),
            _resident((D_MODEL, D_FF)),
            _resident((D_MODEL, D_FF)),
            _resident((D_FF, D_MODEL)),
            _resident((1, D_MODEL)),
        ],
        out_specs=pl.BlockSpec((tm, D_MODEL), lambda i: (i, 0)),
        out_shape=jax.ShapeDtypeStruct((n, D_MODEL), F32),
        compiler_params=pltpu.CompilerParams(
            dimension_semantics=("arbitrary",), vmem_limit_bytes=VMEM_LIMIT),
        name="ffn",
    )(x2d, pre_g, wg, wu, wd, post_g)


def _qkv_kernel(x_ref, g_ref, w_ref, na_ref, d1_ref, d4_ref, d16_ref, scr_ref, *, tm):
    h = _rms(x_ref[0], g_ref[...]).astype(BF16)
    y = jnp.dot(h, w_ref[...], preferred_element_type=F32)
    na_ref[0] = y[:, :D_GROUP].astype(BF16)
    yd = y[:, D_GROUP:]
    d1_ref[0] = yd.astype(BF16)
    n_slabs = D_GROUP // LANES
    for s in range(n_slabs):
        scr_ref[s] = yd[:, s * LANES:(s + 1) * LANES]
    for s in range(n_slabs):
        for r in range(4):
            d4_ref[0, r, :, s * LANES:(s + 1) * LANES] = (
                scr_ref[s, pl.ds(r, tm // 4, stride=4), :].astype(BF16))
        for r in range(16):
            d16_ref[0, r, :, s * LANES:(s + 1) * LANES] = (
                scr_ref[s, pl.ds(r, tm // 16, stride=16), :].astype(BF16))


def _qkv(x, g, w, *, tm=256):
    b, s, _ = x.shape
    return pl.pallas_call(
        functools.partial(_qkv_kernel, tm=tm),
        grid=(b, s // tm),
        in_specs=[
            pl.BlockSpec((1, tm, D_MODEL), lambda bi, i: (bi, i, 0)),
            _resident((1, D_MODEL)),
            _resident((D_MODEL, 2 * D_GROUP)),
        ],
        out_specs=[
            pl.BlockSpec((1, tm, D_GROUP), lambda bi, i: (bi, i, 0)),
            pl.BlockSpec((1, tm, D_GROUP), lambda bi, i: (bi, i, 0)),
            pl.BlockSpec((1, 4, tm // 4, D_GROUP), lambda bi, i: (bi, 0, i, 0)),
            pl.BlockSpec((1, 16, tm // 16, D_GROUP), lambda bi, i: (bi, 0, i, 0)),
        ],
        out_shape=[
            jax.ShapeDtypeStruct((b, s, D_GROUP), BF16),
            jax.ShapeDtypeStruct((b, s, D_GROUP), BF16),
            jax.ShapeDtypeStruct((b, 4, s // 4, D_GROUP), BF16),
            jax.ShapeDtypeStruct((b, 16, s // 16, D_GROUP), BF16),
        ],
        scratch_shapes=[pltpu.VMEM((D_GROUP // LANES, tm, LANES), F32)],
        compiler_params=pltpu.CompilerParams(
            dimension_semantics=("arbitrary", "arbitrary"), vmem_limit_bytes=VMEM_LIMIT),
        name="qkv",
    )(x, g, w)


def _block_attention(q, k, v, table):
    m_rows = q.shape[0]
    lane = lax.broadcasted_iota(jnp.int32, (1, LANES), 1)
    second = lane >= HEAD_DIM
    zero = jnp.zeros_like(q)
    q2 = jnp.concatenate([jnp.where(second, zero, q), jnp.where(second, q, zero)], axis=0)
    s = lax.dot_general(q2, k, (((1,), (1,)), ((), ())), preferred_element_type=F32) + table
    m = jnp.max(s, axis=-1, keepdims=True)
    p = jnp.exp(s - m)
    l = jnp.sum(p, axis=-1, keepdims=True)
    pv = jnp.dot(p.astype(BF16), v, preferred_element_type=F32)
    pick = lambda a: jnp.where(second, a[m_rows:], a[:m_rows])
    return pick(m), pick(l), pick(pv)


def _na_kernel(q_ref, k_ref, v_ref, tbl_ref, o_ref, *, unroll):
    rows = q_ref.shape[1] // GRID_W
    n_keys = NA_KH * GRID_W

    def row(r):
        kr0 = jnp.clip(r - NA_KH // 2, 0, rows - NA_KH)
        q0 = pl.multiple_of(r * GRID_W, GRID_W)
        k0 = pl.multiple_of(kr0 * GRID_W, GRID_W)
        _, l_t, acc_t = _block_attention(
            q_ref[0, pl.ds(q0, GRID_W), :], k_ref[0, pl.ds(k0, n_keys), :],
            v_ref[0, pl.ds(k0, n_keys), :], tbl_ref[0, r - kr0])
        o_ref[0, pl.ds(q0, GRID_W), :] = acc_t / l_t

    def step(i, carry):
        for u in range(unroll):
            row(i * unroll + u)
        return carry

    lax.fori_loop(0, rows // unroll, step, 0)


def _na(qkv_na, tbl, *, unroll=4):
    b, s, _ = qkv_na.shape
    blk = lambda off: pl.BlockSpec((1, s, LANES), lambda bi, p: (bi, 0, off + p))
    return pl.pallas_call(
        functools.partial(_na_kernel, unroll=unroll),
        grid=(b, N_PAIRS),
        in_specs=[
            blk(0), blk(N_PAIRS), blk(2 * N_PAIRS),
            pl.BlockSpec((1,) + tbl.shape[1:], lambda bi, p: (p, 0, 0, 0)),
        ],
        out_specs=pl.BlockSpec((1, s, LANES), lambda bi, p: (bi, 0, p)),
        out_shape=jax.ShapeDtypeStruct((b, s, D_NA), F32),
        compiler_params=pltpu.CompilerParams(
            dimension_semantics=("arbitrary", "arbitrary"), vmem_limit_bytes=VMEM_LIMIT),
        name="na_attn",
    )(qkv_na, qkv_na, qkv_na, tbl)


def _dil_kernel(q1_ref, k1_ref, v1_ref, q4_ref, k4_ref, v4_ref, q16_ref, k16_ref, v16_ref,
                tbl_ref, o_ref, m_ref, l_ref, acc_ref, *, unroll):
    s_len = q1_ref.shape[1]
    qb = DIL_QBLOCK
    kb = DIL_KBLOCK

    def window(i, n_blocks):
        q0 = pl.multiple_of(i * qb, qb)
        w0 = pl.multiple_of(jnp.clip(q0 - DIL_RADIUS, 0, n_blocks * qb - kb), DIL_RADIUS)
        var = jnp.where(i == 0, 0, jnp.where(i == n_blocks - 1, 2, 1))
        return q0, w0, var

    def loop(n, body):
        def step(j, carry):
            for u in range(unroll):
                body(j * unroll + u)
            return carry
        lax.fori_loop(0, n // unroll, step, 0)

    nb1 = s_len // qb

    def p1(i):
        q0, w0, var = window(i, nb1)
        m_t, l_t, acc_t = _block_attention(
            q1_ref[0, pl.ds(q0, qb), :], k1_ref[0, pl.ds(w0, kb), :],
            v1_ref[0, pl.ds(w0, kb), :], tbl_ref[0, 0, var])
        m_ref[pl.ds(q0, qb), :] = m_t
        l_ref[pl.ds(q0, qb), :] = l_t
        acc_ref[pl.ds(q0, qb), :] = acc_t

    loop(nb1, p1)

    def strided_pattern(pat, dil, q_ref, k_ref, v_ref):
        nb = s_len // dil // qb

        def body(j):
            r = j // nb
            i = j % nb
            q0, w0, var = window(i, nb)
            m_b, l_b, acc_b = _block_attention(
                q_ref[0, r, pl.ds(q0, qb), :], k_ref[0, r, pl.ds(w0, kb), :],
                v_ref[0, r, pl.ds(w0, kb), :], tbl_ref[0, pat, var])
            rows = pl.ds(r + q0 * dil, qb, stride=dil)
            m_old = m_ref[rows, :]
            m_new = jnp.maximum(m_old, m_b)
            a = jnp.exp(m_old - m_new)
            bb = jnp.exp(m_b - m_new)
            m_ref[rows, :] = m_new
            l_ref[rows, :] = a * l_ref[rows, :] + bb * l_b
            acc_ref[rows, :] = a * acc_ref[rows, :] + bb * acc_b

        loop(dil * nb, body)

    strided_pattern(1, 4, q4_ref, k4_ref, v4_ref)
    strided_pattern(2, 16, q16_ref, k16_ref, v16_ref)

    def fin(i):
        q0 = pl.multiple_of(i * qb, qb)
        o_ref[0, pl.ds(q0, qb), :] = acc_ref[pl.ds(q0, qb), :] / l_ref[pl.ds(q0, qb), :]

    loop(nb1, fin)


def _dil(d1, d4, d16, tbl, *, unroll=4):
    b, s, _ = d1.shape
    blk1 = lambda off: pl.BlockSpec((1, s, LANES), lambda bi, p: (bi, 0, off + p))
    blkd = lambda d, off: pl.BlockSpec((1, d, s // d, LANES), lambda bi, p: (bi, 0, 0, off + p))
    specs = [blk1(0), blk1(N_PAIRS), blk1(2 * N_PAIRS)]
    for d in (4, 16):
        specs += [blkd(d, 0), blkd(d, N_PAIRS), blkd(d, 2 * N_PAIRS)]
    specs.append(pl.BlockSpec((1,) + tbl.shape[1:], lambda bi, p: (p, 0, 0, 0, 0)))
    return pl.pallas_call(
        functools.partial(_dil_kernel, unroll=unroll),
        grid=(b, N_PAIRS),
        in_specs=specs,
        out_specs=pl.BlockSpec((1, s, LANES), lambda bi, p: (bi, 0, p)),
        out_shape=jax.ShapeDtypeStruct((b, s, D_DIL), F32),
        scratch_shapes=[pltpu.VMEM((s, LANES), F32)] * 3,
        compiler_params=pltpu.CompilerParams(
            dimension_semantics=("arbitrary", "arbitrary"), vmem_limit_bytes=VMEM_LIMIT),
        name="dil_attn",
    )(d1, d1, d1, d4, d4, d4, d16, d16, d16, tbl)


def _out_kernel(x_ref, na_ref, dil_ref, g_na_ref, g_dil_ref, w_ref, g_post_ref, o_ref):
    a = _rms(na_ref[...], g_na_ref[...]).astype(BF16)
    d = _rms(dil_ref[...], g_dil_ref[...]).astype(BF16)
    mixed = (jnp.dot(a, w_ref[:D_NA, :], preferred_element_type=F32)
             + jnp.dot(d, w_ref[D_NA:, :], preferred_element_type=F32))
    o_ref[...] = x_ref[...] + _rms(mixed, g_post_ref[...])


def _outproj(x2d, o_na, o_dil, g_na, g_dil, w, g_post, *, tm=512):
    n = x2d.shape[0]
    row = lambda width: pl.BlockSpec((tm, width), lambda i: (i, 0))
    return pl.pallas_call(
        _out_kernel,
        grid=(n // tm,),
        in_specs=[row(D_MODEL), row(D_NA), row(D_DIL), _resident((1, D_NA)),
                  _resident((1, D_DIL)), _resident((D_NA + D_DIL, D_MODEL)),
                  _resident((1, D_MODEL))],
        out_specs=row(D_MODEL),
        out_shape=jax.ShapeDtypeStruct((n, D_MODEL), F32),
        compiler_params=pltpu.CompilerParams(
            dimension_semantics=("arbitrary",), vmem_limit_bytes=VMEM_LIMIT),
        name="outproj",
    )(x2d, o_na, o_dil, g_na, g_dil, w, g_post)


def _toeplitz(vec, n_rows, n_cols):
    p = n_rows + n_cols - 1
    ext = jnp.concatenate([vec, jnp.zeros(vec.shape[:-1] + (1,), vec.dtype)], axis=-1)
    tiled = jnp.tile(ext, (1,) * (vec.ndim - 1) + (n_rows,))[..., :n_rows * p]
    return tiled.reshape(vec.shape[:-1] + (n_rows, p))[..., n_rows - 1:]


def _pair_rows(tbl):
    lead = tbl.shape[1:-2]
    m, n = tbl.shape[-2:]
    t = tbl.reshape((N_PAIRS, HEADS_PER_TILE) + lead + (m, n))
    t = jnp.moveaxis(t, 1, -3)
    return t.reshape((N_PAIRS,) + lead + (HEADS_PER_TILE * m, n))


def _t5_bucket(rel):
    nb = T5_BUCKETS // 2
    max_exact = nb // 2
    n = jnp.abs(rel)
    large = max_exact + (jnp.log(jnp.maximum(n, 1).astype(F32) / max_exact)
                         / math.log(T5_MAX_DIST / max_exact) * (nb - max_exact)).astype(jnp.int32)
    large = jnp.minimum(large, nb - 1)
    return jnp.where(rel > 0, nb, 0) + jnp.where(n < max_exact, n, large)


def _na_tables(rel_bias):
    pad = GRID_W - NA_KW
    vec = jnp.pad(rel_bias.astype(F32), ((0, 0), (0, 0), (pad, pad)))
    col = _toeplitz(vec, GRID_W, GRID_W)
    qc = np.arange(GRID_W)[:, None]
    kc = np.arange(GRID_W)[None, :]
    qs = np.clip(qc - NA_KW // 2, 0, GRID_W - NA_KW)
    valid = (kc >= qs) & (kc < qs + NA_KW)
    col = jnp.where(valid, col, -jnp.inf)
    variants = []
    for v in range(NA_KH):
        blk = col[:, NA_KH - 1 - v:2 * NA_KH - 1 - v]
        variants.append(jnp.swapaxes(blk, 1, 2).reshape(N_HEADS_NA, GRID_W, NA_KH * GRID_W))
    return _pair_rows(jnp.stack(variants, axis=1))


def _dil_tables(t5_bias):
    span = 2 * DIL_RADIUS + 1
    off = np.arange(span) - DIL_RADIUS
    per_pattern = []
    for _, dil in DIL_PATTERNS:
        bucket = _t5_bucket(jnp.asarray(off * dil, jnp.int32))
        vec = jnp.take(t5_bias.astype(F32), bucket, axis=1)
        per_var = []
        for v in range(3):
            left = DIL_QBLOCK - 1 - DIL_RADIUS + DIL_RADIUS * v
            right = DIL_QBLOCK + DIL_KBLOCK - 1 - span - left
            ext = jnp.pad(vec, ((0, 0), (left, right)), constant_values=-jnp.inf)
            per_var.append(_toeplitz(ext, DIL_QBLOCK, DIL_KBLOCK))
        per_pattern.append(jnp.stack(per_var, axis=1))
    return _pair_rows(jnp.stack(per_pattern, axis=1))


def _group_columns(w_qkv):
    w = w_qkv.reshape(D_MODEL, 3, N_HEADS_NA + N_HEADS_DIL, HEAD_DIM)
    w = w * jnp.asarray([HEAD_DIM ** -0.5, 1.0, 1.0], F32)[None, :, None, None]
    na = w[:, :, :N_HEADS_NA].reshape(D_MODEL, D_GROUP)
    dil = w[:, :, N_HEADS_NA:].reshape(D_MODEL, D_GROUP)
    return jnp.concatenate([na, dil], axis=1).astype(BF16)


def kernel(x, ffn1_pre_g, ffn1_w_gate, ffn1_w_up, ffn1_w_down, ffn1_post_g, mix_pre_g, w_qkv,
           na_rel_bias, t5_rel_bias, na_out_g, dil_out_g, w_out, mix_post_g, ffn2_pre_g,
           ffn2_w_gate, ffn2_w_up, ffn2_w_down, ffn2_post_g):
    b, s, d = x.shape
    depth = ffn1_pre_g.shape[0]
    dil_tbl = _dil_tables(t5_rel_bias)
    for l in range(depth):
        x2d = _ffn(x.reshape(b * s, d), ffn1_pre_g[l][None], ffn1_w_gate[l].astype(BF16),
                   ffn1_w_up[l].astype(BF16), ffn1_w_down[l].astype(BF16), ffn1_post_g[l][None])
        qkv_na, d1, d4, d16 = _qkv(x2d.reshape(b, s, d), mix_pre_g[l][None],
                                   _group_columns(w_qkv[l]))
        o_na = _na(qkv_na, _na_tables(na_rel_bias[l]))
        o_dil = _dil(d1, d4, d16, dil_tbl)
        x2d = _outproj(x2d, o_na.reshape(b * s, D_NA), o_dil.reshape(b * s, D_DIL),
                       na_out_g[l][None], dil_out_g[l][None], w_out[l].astype(BF16),
                       mix_post_g[l][None])
        x2d = _ffn(x2d, ffn2_pre_g[l][None], ffn2_w_gate[l].astype(BF16),
                   ffn2_w_up[l].astype(BF16), ffn2_w_down[l].astype(BF16), ffn2_post_g[l][None])
        x = x2d.reshape(b, s, d)
    return x
```

```python
import functools
import math

import jax
import jax.numpy as jnp
import numpy as np
from jax import lax
from jax.experimental import pallas as pl
from jax.experimental.pallas import tpu as pltpu

D_MODEL = 1024
HEAD_DIM = 64
N_HEADS_NA = 8
N_HEADS_DIL = 8
D_NA = N_HEADS_NA * HEAD_DIM
D_DIL = N_HEADS_DIL * HEAD_DIM
D_FF = 2816
GRID_W = 64
NA_KH = 8
NA_KW = 16
DIL_PATTERNS = ((128, 1), (512, 4), (2048, 16))
DIL_QBLOCK = 128
DIL_RADIUS = 64
DIL_KBLOCK = DIL_QBLOCK + 2 * DIL_RADIUS
T5_BUCKETS = 32
T5_MAX_DIST = 1024
NORM_EPS = 1e-6

LANES = 128
HEADS_PER_TILE = LANES // HEAD_DIM
N_PAIRS = N_HEADS_NA // HEADS_PER_TILE
D_GROUP = 3 * D_NA
VMEM_LIMIT = 56 * 1024 * 1024

BF16 = jnp.bfloat16
F32 = jnp.float32


def _rms(x, g):
    return x * lax.rsqrt(jnp.mean(x * x, axis=-1, keepdims=True) + NORM_EPS) * g


def _resident(shape):
    return pl.BlockSpec(shape, lambda *_: (0,) * len(shape), pipeline_mode=pl.Buffered(1))


def _ffn_kernel(x_ref, pre_g_ref, wg_ref, wu_ref, wd_ref, post_g_ref, o_ref, *, n_chunks):
    x = x_ref[...]
    h = _rms(x, pre_g_ref[...]).astype(BF16)
    ck = D_FF // n_chunks
    y = None
    for c in range(n_chunks):
        g = jnp.dot(h, wg_ref[:, c * ck:(c + 1) * ck], preferred_element_type=F32)
        u = jnp.dot(h, wu_ref[:, c * ck:(c + 1) * ck], preferred_element_type=F32)
        a = (g * jax.nn.sigmoid(g) * u).astype(BF16)
        part = jnp.dot(a, wd_ref[c * ck:(c + 1) * ck, :], preferred_element_type=F32)
        y = part if y is None else y + part
    o_ref[...] = x + 0.5 * _rms(y, post_g_ref[...])


def _ffn(x2d, pre_g, wg, wu, wd, post_g, *, tm=256, n_chunks=2):
    n = x2d.shape[0]
    return pl.pallas_call(
        functools.partial(_ffn_kernel, n_chunks=n_chunks),
        grid=(n // tm,),
        in_specs=[
            pl.BlockSpec((tm, D_MODEL), lambda i: (i, 0)),
            _resident((1, D_MODEL)),
            _resident((D_MODEL, D_FF)),
            _resident((D_MODEL, D_FF)),
            _resident((D_FF, D_MODEL)),
            _resident((1, D_MODEL)),
        ],
        out_specs=pl.BlockSpec((tm, D_MODEL), lambda i: (i, 0)),
        out_shape=jax.ShapeDtypeStruct((n, D_MODEL), F32),
        compiler_params=pltpu.CompilerParams(
            dimension_semantics=("arbitrary",), vmem_limit_bytes=VMEM_LIMIT),
        name="ffn",
    )(x2d, pre_g, wg, wu, wd, post_g)


def _qkv_kernel(x_ref, g_ref, w_ref, na_ref, d1_ref, d4_ref, d16_ref, scr_ref, *, tm):
    h = _rms(x_ref[0], g_ref[...]).astype(BF16)
    y = jnp.dot(h, w_ref[...], preferred_element_type=F32)
    na_ref[0] = y[:, :D_GROUP].astype(BF16)
    yd = y[:, D_GROUP:]
    d1_ref[0] = yd.astype(BF16)
    n_slabs = D_GROUP // LANES
    for s in range(n_slabs):
        scr_ref[s] = yd[:, s * LANES:(s + 1) * LANES]
    for s in range(n_slabs):
        for r in range(4):
            d4_ref[0, r, :, s * LANES:(s + 1) * LANES] = (
                scr_ref[s, pl.ds(r, tm // 4, stride=4), :].astype(BF16))
        for r in range(16):
            d16_ref[0, r, :, s * LANES:(s + 1) * LANES] = (
                scr_ref[s, pl.ds(r, tm // 16, stride=16), :].astype(BF16))


def _qkv(x, g, w, *, tm=256):
    b, s, _ = x.shape
    return pl.pallas_call(
        functools.partial(_qkv_kernel, tm=tm),
        grid=(b, s // tm),
        in_specs=[
            pl.BlockSpec((1, tm, D_MODEL), lambda bi, i: (bi, i, 0)),
            _resident((1, D_MODEL)),
            _resident((D_MODEL, 2 * D_GROUP)),
        ],
        out_specs=[
            pl.BlockSpec((1, tm, D_GROUP), lambda bi, i: (bi, i, 0)),
            pl.BlockSpec((1, tm, D_GROUP), lambda bi, i: (bi, i, 0)),
            pl.BlockSpec((1, 4, tm // 4, D_GROUP), lambda bi, i: (bi, 0, i, 0)),
            pl.BlockSpec((1, 16, tm // 16, D_GROUP), lambda bi, i: (bi, 0, i, 0)),
        ],
        out_shape=[
            jax.ShapeDtypeStruct((b, s, D_GROUP), BF16),
            jax.ShapeDtypeStruct((b, s, D_GROUP), BF16),
            jax.ShapeDtypeStruct((b, 4, s // 4, D_GROUP), BF16),
            jax.ShapeDtypeStruct((b, 16, s // 16, D_GROUP), BF16),
        ],
        scratch_shapes=[pltpu.VMEM((D_GROUP // LANES, tm, LANES), F32)],
        compiler_params=pltpu.CompilerParams(
            dimension_semantics=("arbitrary", "arbitrary"), vmem_limit_bytes=VMEM_LIMIT),
        name="qkv",
    )(x, g, w)


def _attend_blocks(blocks):
    lane = lax.broadcasted_iota(jnp.int32, (1, LANES), 1)
    second = lane >= HEAD_DIM
    scores = []
    for q, k, _, table in blocks:
        zero = jnp.zeros_like(q)
        q2 = jnp.concatenate([jnp.where(second, zero, q), jnp.where(second, q, zero)], axis=0)
        scores.append(lax.dot_general(q2, k, (((1,), (1,)), ((), ())),
                                      preferred_element_type=F32) + table)
    probs = []
    for s in scores:
        m = jnp.max(s, axis=-1, keepdims=True)
        p = jnp.exp(s - m)
        probs.append((m, jnp.sum(p, axis=-1, keepdims=True), p.astype(BF16)))
    out = []
    for (q, _, v, _), (m, l, p) in zip(blocks, probs):
        m_rows = q.shape[0]
        pv = jnp.dot(p, v, preferred_element_type=F32)
        pick = lambda a, m_rows=m_rows: jnp.where(second, a[m_rows:], a[:m_rows])
        out.append((pick(m), pick(l), pick(pv)))
    return out


def _na_kernel(q_ref, k_ref, v_ref, tbl_ref, o_ref, *, unroll):
    rows = q_ref.shape[1] // GRID_W
    n_keys = NA_KH * GRID_W

    def step(i, carry):
        blocks, starts = [], []
        for u in range(unroll):
            r = i * unroll + u
            kr0 = jnp.clip(r - NA_KH // 2, 0, rows - NA_KH)
            q0 = pl.multiple_of(r * GRID_W, GRID_W)
            k0 = pl.multiple_of(kr0 * GRID_W, GRID_W)
            starts.append(q0)
            blocks.append((q_ref[0, pl.ds(q0, GRID_W), :], k_ref[0, pl.ds(k0, n_keys), :],
                           v_ref[0, pl.ds(k0, n_keys), :], tbl_ref[0, r - kr0]))
        for q0, (_, l_t, acc_t) in zip(starts, _attend_blocks(blocks)):
            o_ref[0, pl.ds(q0, GRID_W), :] = acc_t / l_t
        return carry

    lax.fori_loop(0, rows // unroll, step, 0)


def _na(qkv_na, tbl, *, unroll=8):
    b, s, _ = qkv_na.shape
    blk = lambda off: pl.BlockSpec((1, s, LANES), lambda bi, p: (bi, 0, off + p))
    return pl.pallas_call(
        functools.partial(_na_kernel, unroll=unroll),
        grid=(b, N_PAIRS),
        in_specs=[
            blk(0), blk(N_PAIRS), blk(2 * N_PAIRS),
            pl.BlockSpec((1,) + tbl.shape[1:], lambda bi, p: (p, 0, 0, 0)),
        ],
        out_specs=pl.BlockSpec((1, s, LANES), lambda bi, p: (bi, 0, p)),
        out_shape=jax.ShapeDtypeStruct((b, s, D_NA), F32),
        compiler_params=pltpu.CompilerParams(
            dimension_semantics=("arbitrary", "arbitrary"), vmem_limit_bytes=VMEM_LIMIT),
        name="na_attn",
    )(qkv_na, qkv_na, qkv_na, tbl)


def _dil_kernel(q1_ref, k1_ref, v1_ref, q4_ref, k4_ref, v4_ref, q16_ref, k16_ref, v16_ref,
                tbl_ref, o_ref, m_ref, l_ref, acc_ref, *, unroll):
    s_len = q1_ref.shape[1]
    qb = DIL_QBLOCK
    kb = DIL_KBLOCK

    def window(i, n_blocks):
        q0 = pl.multiple_of(i * qb, qb)
        w0 = pl.multiple_of(jnp.clip(q0 - DIL_RADIUS, 0, n_blocks * qb - kb), DIL_RADIUS)
        var = jnp.where(i == 0, 0, jnp.where(i == n_blocks - 1, 2, 1))
        return q0, w0, var

    def merge(rows, m_b, l_b, acc_b):
        m_old = m_ref[rows, :]
        m_new = jnp.maximum(m_old, m_b)
        a = jnp.exp(m_old - m_new)
        bb = jnp.exp(m_b - m_new)
        return m_new, a * l_ref[rows, :] + bb * l_b, a * acc_ref[rows, :] + bb * acc_b

    def strided_pattern(pat, dil, q_ref, k_ref, v_ref, first):
        nb = s_len // dil // qb

        def body(j, carry):
            blocks, row_sets = [], []
            for u in range(unroll):
                idx = j * unroll + u
                r = idx // nb
                q0, w0, var = window(idx % nb, nb)
                row_sets.append(pl.ds(r + q0 * dil, qb, stride=dil))
                blocks.append((q_ref[0, r, pl.ds(q0, qb), :], k_ref[0, r, pl.ds(w0, kb), :],
                               v_ref[0, r, pl.ds(w0, kb), :], tbl_ref[0, pat, var]))
            for rows, pieces in zip(row_sets, _attend_blocks(blocks)):
                m_n, l_n, acc_n = pieces if first else merge(rows, *pieces)
                m_ref[rows, :] = m_n
                l_ref[rows, :] = l_n
                acc_ref[rows, :] = acc_n
            return carry

        lax.fori_loop(0, dil * nb // unroll, body, 0)

    strided_pattern(2, 16, q16_ref, k16_ref, v16_ref, True)
    strided_pattern(1, 4, q4_ref, k4_ref, v4_ref, False)

    nb1 = s_len // qb

    def p1(j, carry):
        blocks, starts = [], []
        for u in range(unroll):
            q0, w0, var = window(j * unroll + u, nb1)
            starts.append(q0)
            blocks.append((q1_ref[0, pl.ds(q0, qb), :], k1_ref[0, pl.ds(w0, kb), :],
                           v1_ref[0, pl.ds(w0, kb), :], tbl_ref[0, 0, var]))
        for q0, pieces in zip(starts, _attend_blocks(blocks)):
            _, l_n, acc_n = merge(pl.ds(q0, qb), *pieces)
            o_ref[0, pl.ds(q0, qb), :] = acc_n / l_n
        return carry

    lax.fori_loop(0, nb1 // unroll, p1, 0)


def _dil(d1, d4, d16, tbl, *, unroll=4):
    b, s, _ = d1.shape
    blk1 = lambda off: pl.BlockSpec((1, s, LANES), lambda bi, p: (bi, 0, off + p))
    blkd = lambda d, off: pl.BlockSpec((1, d, s // d, LANES), lambda bi, p: (bi, 0, 0, off + p))
    specs = [blk1(0), blk1(N_PAIRS), blk1(2 * N_PAIRS)]
    for d in (4, 16):
        specs += [blkd(d, 0), blkd(d, N_PAIRS), blkd(d, 2 * N_PAIRS)]
    specs.append(pl.BlockSpec((1,) + tbl.shape[1:], lambda bi, p: (p, 0, 0, 0, 0)))
    return pl.pallas_call(
        functools.partial(_dil_kernel, unroll=unroll),
        grid=(b, N_PAIRS),
        in_specs=specs,
        out_specs=pl.BlockSpec((1, s, LANES), lambda bi, p: (bi, 0, p)),
        out_shape=jax.ShapeDtypeStruct((b, s, D_DIL), F32),
        scratch_shapes=[pltpu.VMEM((s, LANES), F32)] * 3,
        compiler_params=pltpu.CompilerParams(
            dimension_semantics=("arbitrary", "arbitrary"), vmem_limit_bytes=VMEM_LIMIT),
        name="dil_attn",
    )(d1, d1, d1, d4, d4, d4, d16, d16, d16, tbl)


def _out_kernel(x_ref, na_ref, dil_ref, g_na_ref, g_dil_ref, w_ref, g_post_ref, o_ref):
    a = _rms(na_ref[...], g_na_ref[...]).astype(BF16)
    d = _rms(dil_ref[...], g_dil_ref[...]).astype(BF16)
    mixed = (jnp.dot(a, w_ref[:D_NA, :], preferred_element_type=F32)
             + jnp.dot(d, w_ref[D_NA:, :], preferred_element_type=F32))
    o_ref[...] = x_ref[...] + _rms(mixed, g_post_ref[...])


def _outproj(x2d, o_na, o_dil, g_na, g_dil, w, g_post, *, tm=512):
    n = x2d.shape[0]
    row = lambda width: pl.BlockSpec((tm, width), lambda i: (i, 0))
    return pl.pallas_call(
        _out_kernel,
        grid=(n // tm,),
        in_specs=[row(D_MODEL), row(D_NA), row(D_DIL), _resident((1, D_NA)),
                  _resident((1, D_DIL)), _resident((D_NA + D_DIL, D_MODEL)),
                  _resident((1, D_MODEL))],
        out_specs=row(D_MODEL),
        out_shape=jax.ShapeDtypeStruct((n, D_MODEL), F32),
        compiler_params=pltpu.CompilerParams(
            dimension_semantics=("arbitrary",), vmem_limit_bytes=VMEM_LIMIT),
        name="outproj",
    )(x2d, o_na, o_dil, g_na, g_dil, w, g_post)


def _toeplitz(vec, n_rows, n_cols):
    p = n_rows + n_cols - 1
    ext = jnp.concatenate([vec, jnp.zeros(vec.shape[:-1] + (1,), vec.dtype)], axis=-1)
    tiled = jnp.tile(ext, (1,) * (vec.ndim - 1) + (n_rows,))[..., :n_rows * p]
    return tiled.reshape(vec.shape[:-1] + (n_rows, p))[..., n_rows - 1:]


def _pair_rows(tbl):
    lead = tbl.shape[1:-2]
    m, n = tbl.shape[-2:]
    t = tbl.reshape((N_PAIRS, HEADS_PER_TILE) + lead + (m, n))
    t = jnp.moveaxis(t, 1, -3)
    return t.reshape((N_PAIRS,) + lead + (HEADS_PER_TILE * m, n))


def _t5_bucket(rel):
    nb = T5_BUCKETS // 2
    max_exact = nb // 2
    n = jnp.abs(rel)
    large = max_exact + (jnp.log(jnp.maximum(n, 1).astype(F32) / max_exact)
                         / math.log(T5_MAX_DIST / max_exact) * (nb - max_exact)).astype(jnp.int32)
    large = jnp.minimum(large, nb - 1)
    return jnp.where(rel > 0, nb, 0) + jnp.where(n < max_exact, n, large)


def _na_tables(rel_bias):
    pad = GRID_W - NA_KW
    vec = jnp.pad(rel_bias.astype(F32), ((0, 0), (0, 0), (pad, pad)))
    col = _toeplitz(vec, GRID_W, GRID_W)
    qc = np.arange(GRID_W)[:, None]
    kc = np.arange(GRID_W)[None, :]
    qs = np.clip(qc - NA_KW // 2, 0, GRID_W - NA_KW)
    valid = (kc >= qs) & (kc < qs + NA_KW)
    col = jnp.where(valid, col, -jnp.inf)
    variants = []
    for v in range(NA_KH):
        blk = col[:, NA_KH - 1 - v:2 * NA_KH - 1 - v]
        variants.append(jnp.swapaxes(blk, 1, 2).reshape(N_HEADS_NA, GRID_W, NA_KH * GRID_W))
    return _pair_rows(jnp.stack(variants, axis=1))


def _dil_tables(t5_bias):
    span = 2 * DIL_RADIUS + 1
    off = np.arange(span) - DIL_RADIUS
    per_pattern = []
    for _, dil in DIL_PATTERNS:
        bucket = _t5_bucket(jnp.asarray(off * dil, jnp.int32))
        vec = jnp.take(t5_bias.astype(F32), bucket, axis=1)
        per_var = []
        for v in range(3):
            left = DIL_QBLOCK - 1 - DIL_RADIUS + DIL_RADIUS * v
            right = DIL_QBLOCK + DIL_KBLOCK - 1 - span - left
            ext = jnp.pad(vec, ((0, 0), (left, right)), constant_values=-jnp.inf)
            per_var.append(_toeplitz(ext, DIL_QBLOCK, DIL_KBLOCK))
        per_pattern.append(jnp.stack(per_var, axis=1))
    return _pair_rows(jnp.stack(per_pattern, axis=1))


def _group_columns(w_qkv):
    w = w_qkv.reshape(D_MODEL, 3, N_HEADS_NA + N_HEADS_DIL, HEAD_DIM)
    w = w * jnp.asarray([HEAD_DIM ** -0.5, 1.0, 1.0], F32)[None, :, None, None]
    na = w[:, :, :N_HEADS_NA].reshape(D_MODEL, D_GROUP)
    dil = w[:, :, N_HEADS_NA:].reshape(D_MODEL, D_GROUP)
    return jnp.concatenate([na, dil], axis=1).astype(BF16)


def kernel(x, ffn1_pre_g, ffn1_w_gate, ffn1_w_up, ffn1_w_down, ffn1_post_g, mix_pre_g, w_qkv,
           na_rel_bias, t5_rel_bias, na_out_g, dil_out_g, w_out, mix_post_g, ffn2_pre_g,
           ffn2_w_gate, ffn2_w_up, ffn2_w_down, ffn2_post_g):
    b, s, d = x.shape
    depth = ffn1_pre_g.shape[0]
    dil_tbl = _dil_tables(t5_rel_bias)
    for l in range(depth):
        x2d = _ffn(x.reshape(b * s, d), ffn1_pre_g[l][None], ffn1_w_gate[l].astype(BF16),
                   ffn1_w_up[l].astype(BF16), ffn1_w_down[l].astype(BF16), ffn1_post_g[l][None])
        qkv_na, d1, d4, d16 = _qkv(x2d.reshape(b, s, d), mix_pre_g[l][None],
                                   _group_columns(w_qkv[l]))
        o_na = _na(qkv_na, _na_tables(na_rel_bias[l]))
        o_dil = _dil(d1, d4, d16, dil_tbl)
        x2d = _outproj(x2d, o_na.reshape(b * s, D_NA), o_dil.reshape(b * s, D_DIL),
                       na_out_g[l][None], dil_out_g[l][None], w_out[l].astype(BF16),
                       mix_post_g[l][None])
        x2d = _ffn(x2d, ffn2_pre_g[l][None], ffn2_w_gate[l].astype(BF16),
                   ffn2_w_up[l].astype(BF16), ffn2_w_down[l].astype(BF16), ffn2_post_g[l][None])
        x = x2d.reshape(b, s, d)
    return x
```

```python
import functools
import math

import jax
import jax.numpy as jnp
import numpy as np
from jax import lax
from jax.experimental import pallas as pl
from jax.experimental.pallas import tpu as pltpu

D_MODEL = 1024
HEAD_DIM = 64
N_HEADS_NA = 8
N_HEADS_DIL = 8
D_NA = N_HEADS_NA * HEAD_DIM
D_DIL = N_HEADS_DIL * HEAD_DIM
D_FF = 2816
GRID_W = 64
NA_KH = 8
NA_KW = 16
DIL_PATTERNS = ((128, 1), (512, 4), (2048, 16))
DIL_QBLOCK = 128
DIL_RADIUS = 64
DIL_KBLOCK = DIL_QBLOCK + 2 * DIL_RADIUS
T5_BUCKETS = 32
T5_MAX_DIST = 1024
NORM_EPS = 1e-6

LANES = 128
MXU_COLS = 256
HEADS_PER_TILE = LANES // HEAD_DIM
N_PAIRS = N_HEADS_NA // HEADS_PER_TILE
D_GROUP = 3 * D_NA
VMEM_LIMIT = 56 * 1024 * 1024

BF16 = jnp.bfloat16
F32 = jnp.float32


def _rms(x, g):
    return x * lax.rsqrt(jnp.mean(x * x, axis=-1, keepdims=True) + NORM_EPS) * g


def _resident(shape):
    return pl.BlockSpec(shape, lambda *_: (0,) * len(shape), pipeline_mode=pl.Buffered(1))


def _ffn_kernel(x_ref, pre_g_ref, wg_ref, wu_ref, wd_ref, post_g_ref, o_ref, *, n_chunks):
    x = x_ref[...]
    h = _rms(x, pre_g_ref[...]).astype(BF16)
    ck = D_FF // n_chunks
    y = None
    for c in range(n_chunks):
        g = jnp.dot(h, wg_ref[:, c * ck:(c + 1) * ck], preferred_element_type=F32)
        u = jnp.dot(h, wu_ref[:, c * ck:(c + 1) * ck], preferred_element_type=F32)
        a = (g * jax.nn.sigmoid(g) * u).astype(BF16)
        part = jnp.dot(a, wd_ref[c * ck:(c + 1) * ck, :], preferred_element_type=F32)
        y = part if y is None else y + part
    o_ref[...] = x + 0.5 * _rms(y, post_g_ref[...])


def _ffn(x2d, pre_g, wg, wu, wd, post_g, *, tm=512, n_chunks=1):
    n = x2d.shape[0]
    return pl.pallas_call(
        functools.partial(_ffn_kernel, n_chunks=n_chunks),
        grid=(n // tm,),
        in_specs=[
            pl.BlockSpec((tm, D_MODEL), lambda i: (i, 0)),
            _resident((1, D_MODEL)),
            _resident((D_MODEL, D_FF)),
            _resident((D_MODEL, D_FF)),
            _resident((D_FF, D_MODEL)),
            _resident((1, D_MODEL)),
        ],
        out_specs=pl.BlockSpec((tm, D_MODEL), lambda i: (i, 0)),
        out_shape=jax.ShapeDtypeStruct((n, D_MODEL), F32),
        compiler_params=pltpu.CompilerParams(
            dimension_semantics=("arbitrary",), vmem_limit_bytes=VMEM_LIMIT),
        name="ffn",
    )(x2d, pre_g, wg, wu, wd, post_g)


def _qkv_kernel(x_ref, g_ref, w_ref, na_ref, d1_ref, d4_ref, d16_ref, scr_ref, *, tm):
    h = _rms(x_ref[0], g_ref[...]).astype(BF16)
    for c in range(0, D_GROUP, MXU_COLS):
        y = jnp.dot(h, w_ref[:, D_GROUP + c:D_GROUP + c + MXU_COLS], preferred_element_type=F32)
        d1_ref[0, :, c:c + MXU_COLS] = y.astype(BF16)
        for lo in range(c, c + MXU_COLS, LANES):
            s = lo // LANES
            scr_ref[s] = y[:, lo - c:lo - c + LANES]
            for r in range(4):
                d4_ref[0, r, :, lo:lo + LANES] = (
                    scr_ref[s, pl.ds(r, tm // 4, stride=4), :].astype(BF16))
            for r in range(16):
                d16_ref[0, r, :, lo:lo + LANES] = (
                    scr_ref[s, pl.ds(r, tm // 16, stride=16), :].astype(BF16))
    for c in range(0, D_GROUP, MXU_COLS):
        y = jnp.dot(h, w_ref[:, c:c + MXU_COLS], preferred_element_type=F32)
        na_ref[0, :, c:c + MXU_COLS] = y.astype(BF16)


def _qkv(x, g, w, *, tm=512):
    b, s, _ = x.shape
    return pl.pallas_call(
        functools.partial(_qkv_kernel, tm=tm),
        grid=(b, s // tm),
        in_specs=[
            pl.BlockSpec((1, tm, D_MODEL), lambda bi, i: (bi, i, 0)),
            _resident((1, D_MODEL)),
            _resident((D_MODEL, 2 * D_GROUP)),
        ],
        out_specs=[
            pl.BlockSpec((1, tm, D_GROUP), lambda bi, i: (bi, i, 0)),
            pl.BlockSpec((1, tm, D_GROUP), lambda bi, i: (bi, i, 0)),
            pl.BlockSpec((1, 4, tm // 4, D_GROUP), lambda bi, i: (bi, 0, i, 0)),
            pl.BlockSpec((1, 16, tm // 16, D_GROUP), lambda bi, i: (bi, 0, i, 0)),
        ],
        out_shape=[
            jax.ShapeDtypeStruct((b, s, D_GROUP), BF16),
            jax.ShapeDtypeStruct((b, s, D_GROUP), BF16),
            jax.ShapeDtypeStruct((b, 4, s // 4, D_GROUP), BF16),
            jax.ShapeDtypeStruct((b, 16, s // 16, D_GROUP), BF16),
        ],
        scratch_shapes=[pltpu.VMEM((D_GROUP // LANES, tm, LANES), F32)],
        compiler_params=pltpu.CompilerParams(
            dimension_semantics=("arbitrary", "arbitrary"), vmem_limit_bytes=VMEM_LIMIT),
        name="qkv",
    )(x, g, w)


def _attend_blocks(blocks):
    lane = lax.broadcasted_iota(jnp.int32, (1, LANES), 1)
    second = lane >= HEAD_DIM
    scores = []
    for q, k, _, table in blocks:
        zero = jnp.zeros_like(q)
        q2 = jnp.concatenate([jnp.where(second, zero, q), jnp.where(second, q, zero)], axis=0)
        scores.append(lax.dot_general(q2, k, (((1,), (1,)), ((), ())),
                                      preferred_element_type=F32) + table)
    probs = []
    for s in scores:
        m = jnp.max(s, axis=-1, keepdims=True)
        p = jnp.exp(s - m)
        probs.append((m, jnp.sum(p, axis=-1, keepdims=True), p.astype(BF16)))
    out = []
    for (q, _, v, _), (m, l, p) in zip(blocks, probs):
        m_rows = q.shape[0]
        pv = jnp.dot(p, v, preferred_element_type=F32)
        pick = lambda a, m_rows=m_rows: jnp.where(second, a[m_rows:], a[:m_rows])
        out.append((pick(m), pick(l), pick(pv)))
    return out


def _na_kernel(q_ref, k_ref, v_ref, tbl_ref, o_ref, *, unroll):
    rows = q_ref.shape[1] // GRID_W
    n_keys = NA_KH * GRID_W

    def step(i, carry):
        blocks, starts = [], []
        for u in range(unroll):
            r = i * unroll + u
            kr0 = jnp.clip(r - NA_KH // 2, 0, rows - NA_KH)
            q0 = pl.multiple_of(r * GRID_W, GRID_W)
            k0 = pl.multiple_of(kr0 * GRID_W, GRID_W)
            starts.append(q0)
            blocks.append((q_ref[0, pl.ds(q0, GRID_W), :], k_ref[0, pl.ds(k0, n_keys), :],
                           v_ref[0, pl.ds(k0, n_keys), :], tbl_ref[0, r - kr0]))
        for q0, (_, l_t, acc_t) in zip(starts, _attend_blocks(blocks)):
            o_ref[0, pl.ds(q0, GRID_W), :] = acc_t / l_t
        return carry

    lax.fori_loop(0, rows // unroll, step, 0)


def _na(qkv_na, tbl, *, unroll=8):
    b, s, _ = qkv_na.shape
    blk = lambda off: pl.BlockSpec((1, s, LANES), lambda bi, p: (bi, 0, off + p))
    return pl.pallas_call(
        functools.partial(_na_kernel, unroll=unroll),
        grid=(b, N_PAIRS),
        in_specs=[
            blk(0), blk(N_PAIRS), blk(2 * N_PAIRS),
            pl.BlockSpec((1,) + tbl.shape[1:], lambda bi, p: (p, 0, 0, 0)),
        ],
        out_specs=pl.BlockSpec((1, s, LANES), lambda bi, p: (bi, 0, p)),
        out_shape=jax.ShapeDtypeStruct((b, s, D_NA), F32),
        compiler_params=pltpu.CompilerParams(
            dimension_semantics=("arbitrary", "arbitrary"), vmem_limit_bytes=VMEM_LIMIT),
        name="na_attn",
    )(qkv_na, qkv_na, qkv_na, tbl)


def _dil_kernel(q1_ref, k1_ref, v1_ref, q4_ref, k4_ref, v4_ref, q16_ref, k16_ref, v16_ref,
                tbl_ref, o_ref, m_ref, l_ref, acc_ref, *, unroll):
    s_len = q1_ref.shape[1]
    qb = DIL_QBLOCK
    kb = DIL_KBLOCK

    def window(i, n_blocks):
        q0 = pl.multiple_of(i * qb, qb)
        w0 = pl.multiple_of(jnp.clip(q0 - DIL_RADIUS, 0, n_blocks * qb - kb), DIL_RADIUS)
        var = jnp.where(i == 0, 0, jnp.where(i == n_blocks - 1, 2, 1))
        return q0, w0, var

    def merge(rows, m_b, l_b, acc_b):
        m_old = m_ref[rows, :]
        m_new = jnp.maximum(m_old, m_b)
        a = jnp.exp(m_old - m_new)
        bb = jnp.exp(m_b - m_new)
        return m_new, a * l_ref[rows, :] + bb * l_b, a * acc_ref[rows, :] + bb * acc_b

    def strided_pattern(pat, dil, q_ref, k_ref, v_ref, first):
        nb = s_len // dil // qb

        def body(j, carry):
            blocks, row_sets = [], []
            for u in range(unroll):
                idx = j * unroll + u
                r = idx // nb
                q0, w0, var = window(idx % nb, nb)
                row_sets.append(pl.ds(r + q0 * dil, qb, stride=dil))
                blocks.append((q_ref[0, r, pl.ds(q0, qb), :], k_ref[0, r, pl.ds(w0, kb), :],
                               v_ref[0, r, pl.ds(w0, kb), :], tbl_ref[0, pat, var]))
            for rows, pieces in zip(row_sets, _attend_blocks(blocks)):
                m_n, l_n, acc_n = pieces if first else merge(rows, *pieces)
                m_ref[rows, :] = m_n
                l_ref[rows, :] = l_n
                acc_ref[rows, :] = acc_n
            return carry

        lax.fori_loop(0, dil * nb // unroll, body, 0)

    strided_pattern(2, 16, q16_ref, k16_ref, v16_ref, True)
    strided_pattern(1, 4, q4_ref, k4_ref, v4_ref, False)

    nb1 = s_len // qb

    def p1(j, carry):
        blocks, starts = [], []
        for u in range(unroll):
            q0, w0, var = window(j * unroll + u, nb1)
            starts.append(q0)
            blocks.append((q1_ref[0, pl.ds(q0, qb), :], k1_ref[0, pl.ds(w0, kb), :],
                           v1_ref[0, pl.ds(w0, kb), :], tbl_ref[0, 0, var]))
        for q0, pieces in zip(starts, _attend_blocks(blocks)):
            _, l_n, acc_n = merge(pl.ds(q0, qb), *pieces)
            o_ref[0, pl.ds(q0, qb), :] = acc_n / l_n
        return carry

    lax.fori_loop(0, nb1 // unroll, p1, 0)


def _dil(d1, d4, d16, tbl, *, unroll=4):
    b, s, _ = d1.shape
    blk1 = lambda off: pl.BlockSpec((1, s, LANES), lambda bi, p: (bi, 0, off + p))
    blkd = lambda d, off: pl.BlockSpec((1, d, s // d, LANES), lambda bi, p: (bi, 0, 0, off + p))
    specs = [blk1(0), blk1(N_PAIRS), blk1(2 * N_PAIRS)]
    for d in (4, 16):
        specs += [blkd(d, 0), blkd(d, N_PAIRS), blkd(d, 2 * N_PAIRS)]
    specs.append(pl.BlockSpec((1,) + tbl.shape[1:], lambda bi, p: (p, 0, 0, 0, 0)))
    return pl.pallas_call(
        functools.partial(_dil_kernel, unroll=unroll),
        grid=(b, N_PAIRS),
        in_specs=specs,
        out_specs=pl.BlockSpec((1, s, LANES), lambda bi, p: (bi, 0, p)),
        out_shape=jax.ShapeDtypeStruct((b, s, D_DIL), F32),
        scratch_shapes=[pltpu.VMEM((s, LANES), F32)] * 3,
        compiler_params=pltpu.CompilerParams(
            dimension_semantics=("arbitrary", "arbitrary"), vmem_limit_bytes=VMEM_LIMIT),
        name="dil_attn",
    )(d1, d1, d1, d4, d4, d4, d16, d16, d16, tbl)


def _out_kernel(x_ref, na_ref, dil_ref, g_na_ref, g_dil_ref, w_ref, g_post_ref, o_ref):
    a = _rms(na_ref[...], g_na_ref[...]).astype(BF16)
    d = _rms(dil_ref[...], g_dil_ref[...]).astype(BF16)
    mixed = (jnp.dot(a, w_ref[:D_NA, :], preferred_element_type=F32)
             + jnp.dot(d, w_ref[D_NA:, :], preferred_element_type=F32))
    o_ref[...] = x_ref[...] + _rms(mixed, g_post_ref[...])


def _outproj(x2d, o_na, o_dil, g_na, g_dil, w, g_post, *, tm=512):
    n = x2d.shape[0]
    row = lambda width: pl.BlockSpec((tm, width), lambda i: (i, 0))
    return pl.pallas_call(
        _out_kernel,
        grid=(n // tm,),
        in_specs=[row(D_MODEL), row(D_NA), row(D_DIL), _resident((1, D_NA)),
                  _resident((1, D_DIL)), _resident((D_NA + D_DIL, D_MODEL)),
                  _resident((1, D_MODEL))],
        out_specs=row(D_MODEL),
        out_shape=jax.ShapeDtypeStruct((n, D_MODEL), F32),
        compiler_params=pltpu.CompilerParams(
            dimension_semantics=("arbitrary",), vmem_limit_bytes=VMEM_LIMIT),
        name="outproj",
    )(x2d, o_na, o_dil, g_na, g_dil, w, g_post)


def _toeplitz(vec, n_rows, n_cols):
    p = n_rows + n_cols - 1
    ext = jnp.concatenate([vec, jnp.zeros(vec.shape[:-1] + (1,), vec.dtype)], axis=-1)
    tiled = jnp.tile(ext, (1,) * (vec.ndim - 1) + (n_rows,))[..., :n_rows * p]
    return tiled.reshape(vec.shape[:-1] + (n_rows, p))[..., n_rows - 1:]


def _pair_rows(tbl):
    lead = tbl.shape[1:-2]
    m, n = tbl.shape[-2:]
    t = tbl.reshape((N_PAIRS, HEADS_PER_TILE) + lead + (m, n))
    t = jnp.moveaxis(t, 1, -3)
    return t.reshape((N_PAIRS,) + lead + (HEADS_PER_TILE * m, n))


def _t5_bucket(rel):
    nb = T5_BUCKETS // 2
    max_exact = nb // 2
    n = jnp.abs(rel)
    large = max_exact + (jnp.log(jnp.maximum(n, 1).astype(F32) / max_exact)
                         / math.log(T5_MAX_DIST / max_exact) * (nb - max_exact)).astype(jnp.int32)
    large = jnp.minimum(large, nb - 1)
    return jnp.where(rel > 0, nb, 0) + jnp.where(n < max_exact, n, large)


def _na_tables(rel_bias):
    pad = GRID_W - NA_KW
    vec = jnp.pad(rel_bias.astype(F32), ((0, 0), (0, 0), (pad, pad)))
    col = _toeplitz(vec, GRID_W, GRID_W)
    qc = np.arange(GRID_W)[:, None]
    kc = np.arange(GRID_W)[None, :]
    qs = np.clip(qc - NA_KW // 2, 0, GRID_W - NA_KW)
    valid = (kc >= qs) & (kc < qs + NA_KW)
    col = jnp.where(valid, col, -jnp.inf)
    variants = []
    for v in range(NA_KH):
        blk = col[:, NA_KH - 1 - v:2 * NA_KH - 1 - v]
        variants.append(jnp.swapaxes(blk, 1, 2).reshape(N_HEADS_NA, GRID_W, NA_KH * GRID_W))
    return _pair_rows(jnp.stack(variants, axis=1))


def _dil_tables(t5_bias):
    span = 2 * DIL_RADIUS + 1
    off = np.arange(span) - DIL_RADIUS
    per_pattern = []
    for _, dil in DIL_PATTERNS:
        bucket = _t5_bucket(jnp.asarray(off * dil, jnp.int32))
        vec = jnp.take(t5_bias.astype(F32), bucket, axis=1)
        per_var = []
        for v in range(3):
            left = DIL_QBLOCK - 1 - DIL_RADIUS + DIL_RADIUS * v
            right = DIL_QBLOCK + DIL_KBLOCK - 1 - span - left
            ext = jnp.pad(vec, ((0, 0), (left, right)), constant_values=-jnp.inf)
            per_var.append(_toeplitz(ext, DIL_QBLOCK, DIL_KBLOCK))
        per_pattern.append(jnp.stack(per_var, axis=1))
    return _pair_rows(jnp.stack(per_pattern, axis=1))


def _group_columns(w_qkv):
    w = w_qkv.reshape(D_MODEL, 3, N_HEADS_NA + N_HEADS_DIL, HEAD_DIM)
    w = w * jnp.asarray([HEAD_DIM ** -0.5, 1.0, 1.0], F32)[None, :, None, None]
    na = w[:, :, :N_HEADS_NA].reshape(D_MODEL, D_GROUP)
    dil = w[:, :, N_HEADS_NA:].reshape(D_MODEL, D_GROUP)
    return jnp.concatenate([na, dil], axis=1).astype(BF16)


def kernel(x, ffn1_pre_g, ffn1_w_gate, ffn1_w_up, ffn1_w_down, ffn1_post_g, mix_pre_g, w_qkv,
           na_rel_bias, t5_rel_bias, na_out_g, dil_out_g, w_out, mix_post_g, ffn2_pre_g,
           ffn2_w_gate, ffn2_w_up, ffn2_w_down, ffn2_post_g):
    b, s, d = x.shape
    depth = ffn1_pre_g.shape[0]
    dil_tbl = _dil_tables(t5_rel_bias)
    for l in range(depth):
        x2d = _ffn(x.reshape(b * s, d), ffn1_pre_g[l][None], ffn1_w_gate[l].astype(BF16),
                   ffn1_w_up[l].astype(BF16), ffn1_w_down[l].astype(BF16), ffn1_post_g[l][None])
        qkv_na, d1, d4, d16 = _qkv(x2d.reshape(b, s, d), mix_pre_g[l][None],
                                   _group_columns(w_qkv[l]))
        o_na = _na(qkv_na, _na_tables(na_rel_bias[l]))
        o_dil = _dil(d1, d4, d16, dil_tbl)
        x2d = _outproj(x2d, o_na.reshape(b * s, D_NA), o_dil.reshape(b * s, D_DIL),
                       na_out_g[l][None], dil_out_g[l][None], w_out[l].astype(BF16),
                       mix_post_g[l][None])
        x2d = _ffn(x2d, ffn2_pre_g[l][None], ffn2_w_gate[l].astype(BF16),
                   ffn2_w_up[l].astype(BF16), ffn2_w_down[l].astype(BF16), ffn2_post_g[l][None])
        x = x2d.reshape(b, s, d)
    return x
```

```python
import functools
import math

import jax
import jax.numpy as jnp
import numpy as np
from jax import lax
from jax.experimental import pallas as pl
from jax.experimental.pallas import tpu as pltpu

D_MODEL = 1024
HEAD_DIM = 64
N_HEADS_NA = 8
N_HEADS_DIL = 8
D_NA = N_HEADS_NA * HEAD_DIM
D_DIL = N_HEADS_DIL * HEAD_DIM
D_FF = 2816
GRID_W = 64
NA_KH = 8
NA_KW = 16
DIL_PATTERNS = ((128, 1), (512, 4), (2048, 16))
DIL_QBLOCK = 128
DIL_RADIUS = 64
DIL_KBLOCK = DIL_QBLOCK + 2 * DIL_RADIUS
T5_BUCKETS = 32
T5_MAX_DIST = 1024
NORM_EPS = 1e-6

LANES = 128
MXU_COLS = 256
HEADS_PER_TILE = LANES // HEAD_DIM
N_PAIRS = N_HEADS_NA // HEADS_PER_TILE
D_GROUP = 3 * D_NA
VMEM_LIMIT = 56 * 1024 * 1024

BF16 = jnp.bfloat16
F32 = jnp.float32


def _rms(x, g):
    return x * lax.rsqrt(jnp.mean(x * x, axis=-1, keepdims=True) + NORM_EPS) * g


def _resident(shape):
    return pl.BlockSpec(shape, lambda *_: (0,) * len(shape), pipeline_mode=pl.Buffered(1))


def _half_step_ffn(x, pre_g_ref, wg_ref, wu_ref, wd_ref, post_g_ref):
    h = _rms(x, pre_g_ref[...]).astype(BF16)
    g = jnp.dot(h, wg_ref[...], preferred_element_type=F32)
    u = jnp.dot(h, wu_ref[...], preferred_element_type=F32)
    a = (g * jax.nn.sigmoid(g) * u).astype(BF16)
    y = jnp.dot(a, wd_ref[...], preferred_element_type=F32)
    return x + 0.5 * _rms(y, post_g_ref[...])


def _ffn_kernel(x_ref, *refs):
    o_ref = refs[-1]
    o_ref[...] = _half_step_ffn(x_ref[...], *refs[:-1])


def _mix_ffn_kernel(x_ref, na_ref, dil_ref, g_na_ref, g_dil_ref, w_ref, g_mix_ref, *refs):
    o_ref = refs[-1]
    tiles = lambda ref: jnp.concatenate([ref[t] for t in range(N_PAIRS)], axis=-1)
    a = _rms(tiles(na_ref), g_na_ref[...]).astype(BF16)
    d = _rms(tiles(dil_ref), g_dil_ref[...]).astype(BF16)
    mixed = (jnp.dot(a, w_ref[:D_NA, :], preferred_element_type=F32)
             + jnp.dot(d, w_ref[D_NA:, :], preferred_element_type=F32))
    x = x_ref[...] + _rms(mixed, g_mix_ref[...])
    o_ref[...] = _half_step_ffn(x, *refs[:-1])


def _ffn_specs():
    return [_resident((1, D_MODEL)), _resident((D_MODEL, D_FF)), _resident((D_MODEL, D_FF)),
            _resident((D_FF, D_MODEL)), _resident((1, D_MODEL))]


def _ffn(x, ffn_params, *, tm=512):
    b, s, _ = x.shape
    row = pl.BlockSpec((None, tm, D_MODEL), lambda bi, i: (bi, i, 0))
    return pl.pallas_call(
        _ffn_kernel,
        grid=(b, s // tm),
        in_specs=[row] + _ffn_specs(),
        out_specs=row,
        out_shape=jax.ShapeDtypeStruct((b, s, D_MODEL), F32),
        compiler_params=pltpu.CompilerParams(
            dimension_semantics=("arbitrary", "arbitrary"), vmem_limit_bytes=VMEM_LIMIT),
        name="ffn",
    )(x, *ffn_params)


def _mix_ffn(x, o_na, o_dil, g_na, g_dil, w, g_mix, ffn_params, *, tm=512):
    b, s, _ = x.shape
    row = pl.BlockSpec((None, tm, D_MODEL), lambda bi, i: (bi, i, 0))
    heads = pl.BlockSpec((None, N_PAIRS, tm, LANES), lambda bi, i: (bi, 0, i, 0))
    return pl.pallas_call(
        _mix_ffn_kernel,
        grid=(b, s // tm),
        in_specs=[row, heads, heads, _resident((1, D_NA)), _resident((1, D_DIL)),
                  _resident((D_NA + D_DIL, D_MODEL)), _resident((1, D_MODEL))] + _ffn_specs(),
        out_specs=row,
        out_shape=jax.ShapeDtypeStruct((b, s, D_MODEL), F32),
        compiler_params=pltpu.CompilerParams(
            dimension_semantics=("arbitrary", "arbitrary"), vmem_limit_bytes=VMEM_LIMIT),
        name="mix_ffn",
    )(x, o_na, o_dil, g_na, g_dil, w, g_mix, *ffn_params)


def _qkv_kernel(x_ref, g_ref, w_ref, na_ref, d1_ref, d4_ref, d16_ref, scr_ref, *, tm):
    h = _rms(x_ref[0], g_ref[...]).astype(BF16)
    for c in range(0, D_GROUP, MXU_COLS):
        y = jnp.dot(h, w_ref[:, D_GROUP + c:D_GROUP + c + MXU_COLS], preferred_element_type=F32)
        for lo in range(0, MXU_COLS, LANES):
            t = (c + lo) // LANES
            y_t = y[:, lo:lo + LANES]
            d1_ref[0, t] = y_t.astype(BF16)
            scr_ref[t] = y_t
            for r in range(4):
                d4_ref[0, t, r] = scr_ref[t, pl.ds(r, tm // 4, stride=4), :].astype(BF16)
            for r in range(16):
                d16_ref[0, t, r] = scr_ref[t, pl.ds(r, tm // 16, stride=16), :].astype(BF16)
    for c in range(0, D_GROUP, MXU_COLS):
        y = jnp.dot(h, w_ref[:, c:c + MXU_COLS], preferred_element_type=F32)
        for lo in range(0, MXU_COLS, LANES):
            na_ref[0, (c + lo) // LANES] = y[:, lo:lo + LANES].astype(BF16)


def _qkv(x, g, w, *, tm=512):
    b, s, _ = x.shape
    n_t = D_GROUP // LANES
    return pl.pallas_call(
        functools.partial(_qkv_kernel, tm=tm),
        grid=(b, s // tm),
        in_specs=[
            pl.BlockSpec((1, tm, D_MODEL), lambda bi, i: (bi, i, 0)),
            _resident((1, D_MODEL)),
            _resident((D_MODEL, 2 * D_GROUP)),
        ],
        out_specs=[
            pl.BlockSpec((1, n_t, tm, LANES), lambda bi, i: (bi, 0, i, 0)),
            pl.BlockSpec((1, n_t, tm, LANES), lambda bi, i: (bi, 0, i, 0)),
            pl.BlockSpec((1, n_t, 4, tm // 4, LANES), lambda bi, i: (bi, 0, 0, i, 0)),
            pl.BlockSpec((1, n_t, 16, tm // 16, LANES), lambda bi, i: (bi, 0, 0, i, 0)),
        ],
        out_shape=[
            jax.ShapeDtypeStruct((b, n_t, s, LANES), BF16),
            jax.ShapeDtypeStruct((b, n_t, s, LANES), BF16),
            jax.ShapeDtypeStruct((b, n_t, 4, s // 4, LANES), BF16),
            jax.ShapeDtypeStruct((b, n_t, 16, s // 16, LANES), BF16),
        ],
        scratch_shapes=[pltpu.VMEM((n_t, tm, LANES), F32)],
        compiler_params=pltpu.CompilerParams(
            dimension_semantics=("arbitrary", "arbitrary"), vmem_limit_bytes=VMEM_LIMIT),
        name="qkv",
    )(x, g, w)


def _attend_blocks(blocks):
    lane = lax.broadcasted_iota(jnp.int32, (1, LANES), 1)
    second = lane >= HEAD_DIM
    scores = []
    for q, k, _, table in blocks:
        zero = jnp.zeros_like(q)
        q2 = jnp.concatenate([jnp.where(second, zero, q), jnp.where(second, q, zero)], axis=0)
        scores.append(lax.dot_general(q2, k, (((1,), (1,)), ((), ())),
                                      preferred_element_type=F32) + table)
    probs = []
    for s in scores:
        m = jnp.max(s, axis=-1, keepdims=True)
        p = jnp.exp(s - m)
        probs.append((m, jnp.sum(p, axis=-1, keepdims=True), p.astype(BF16)))
    out = []
    for (q, _, v, _), (m, l, p) in zip(blocks, probs):
        m_rows = q.shape[0]
        pv = jnp.dot(p, v, preferred_element_type=F32)
        pick = lambda a, m_rows=m_rows: jnp.where(second, a[m_rows:], a[:m_rows])
        out.append((pick(m), pick(l), pick(pv)))
    return out


def _na_kernel(q_ref, k_ref, v_ref, tbl_ref, o_ref, *, unroll):
    rows = q_ref.shape[0] // GRID_W
    n_keys = NA_KH * GRID_W

    def step(i, carry):
        blocks, starts = [], []
        for u in range(unroll):
            r = i * unroll + u
            kr0 = jnp.clip(r - NA_KH // 2, 0, rows - NA_KH)
            q0 = pl.multiple_of(r * GRID_W, GRID_W)
            k0 = pl.multiple_of(kr0 * GRID_W, GRID_W)
            starts.append(q0)
            blocks.append((q_ref[pl.ds(q0, GRID_W), :], k_ref[pl.ds(k0, n_keys), :],
                           v_ref[pl.ds(k0, n_keys), :], tbl_ref[r - kr0]))
        for q0, (_, l_t, acc_t) in zip(starts, _attend_blocks(blocks)):
            o_ref[pl.ds(q0, GRID_W), :] = acc_t / l_t
        return carry

    lax.fori_loop(0, rows // unroll, step, 0)


def _na(qkv_na, tbl, *, unroll=8):
    b, _, s, _ = qkv_na.shape
    blk = lambda off: pl.BlockSpec((None, None, s, LANES), lambda p, bi: (bi, off + p, 0, 0))
    return pl.pallas_call(
        functools.partial(_na_kernel, unroll=unroll),
        grid=(N_PAIRS, b),
        in_specs=[
            blk(0), blk(N_PAIRS), blk(2 * N_PAIRS),
            pl.BlockSpec((None,) + tbl.shape[1:], lambda p, bi: (p, 0, 0, 0)),
        ],
        out_specs=pl.BlockSpec((None, None, s, LANES), lambda p, bi: (bi, p, 0, 0)),
        out_shape=jax.ShapeDtypeStruct((b, N_PAIRS, s, LANES), F32),
        compiler_params=pltpu.CompilerParams(
            dimension_semantics=("arbitrary", "arbitrary"), vmem_limit_bytes=VMEM_LIMIT),
        name="na_attn",
    )(qkv_na, qkv_na, qkv_na, tbl)


def _dil_kernel(q1_ref, k1_ref, v1_ref, q4_ref, k4_ref, v4_ref, q16_ref, k16_ref, v16_ref,
                tbl_ref, o_ref, m_ref, l_ref, acc_ref, *, unroll):
    s_len = q1_ref.shape[0]
    qb = DIL_QBLOCK
    kb = DIL_KBLOCK

    def window(i, n_blocks):
        q0 = pl.multiple_of(i * qb, qb)
        w0 = pl.multiple_of(jnp.clip(q0 - DIL_RADIUS, 0, n_blocks * qb - kb), DIL_RADIUS)
        var = jnp.where(i == 0, 0, jnp.where(i == n_blocks - 1, 2, 1))
        return q0, w0, var

    def merge(rows, m_b, l_b, acc_b):
        m_old = m_ref[rows, :]
        m_new = jnp.maximum(m_old, m_b)
        a = jnp.exp(m_old - m_new)
        bb = jnp.exp(m_b - m_new)
        return m_new, a * l_ref[rows, :] + bb * l_b, a * acc_ref[rows, :] + bb * acc_b

    def strided_pattern(pat, dil, q_ref, k_ref, v_ref, first):
        nb = s_len // dil // qb

        def body(j, carry):
            blocks, row_sets = [], []
            for u in range(unroll):
                idx = j * unroll + u
                r = idx // nb
                q0, w0, var = window(idx % nb, nb)
                row_sets.append(pl.ds(r + q0 * dil, qb, stride=dil))
                blocks.append((q_ref[r, pl.ds(q0, qb), :], k_ref[r, pl.ds(w0, kb), :],
                               v_ref[r, pl.ds(w0, kb), :], tbl_ref[pat, var]))
            for rows, pieces in zip(row_sets, _attend_blocks(blocks)):
                m_n, l_n, acc_n = pieces if first else merge(rows, *pieces)
                m_ref[rows, :] = m_n
                l_ref[rows, :] = l_n
                acc_ref[rows, :] = acc_n
            return carry

        lax.fori_loop(0, dil * nb // unroll, body, 0)

    strided_pattern(2, 16, q16_ref, k16_ref, v16_ref, True)
    strided_pattern(1, 4, q4_ref, k4_ref, v4_ref, False)

    nb1 = s_len // qb

    def p1(j, carry):
        blocks, starts = [], []
        for u in range(unroll):
            q0, w0, var = window(j * unroll + u, nb1)
            starts.append(q0)
            blocks.append((q1_ref[pl.ds(q0, qb), :], k1_ref[pl.ds(w0, kb), :],
                           v1_ref[pl.ds(w0, kb), :], tbl_ref[0, var]))
        for q0, pieces in zip(starts, _attend_blocks(blocks)):
            _, l_n, acc_n = merge(pl.ds(q0, qb), *pieces)
            o_ref[pl.ds(q0, qb), :] = acc_n / l_n
        return carry

    lax.fori_loop(0, nb1 // unroll, p1, 0)


def _dil(d1, d4, d16, tbl, *, unroll=4):
    b, _, s, _ = d1.shape
    blk1 = lambda off: pl.BlockSpec((None, None, s, LANES), lambda p, bi: (bi, off + p, 0, 0))
    blkd = lambda d, off: pl.BlockSpec((None, None, d, s // d, LANES),
                                       lambda p, bi: (bi, off + p, 0, 0, 0))
    specs = [blk1(0), blk1(N_PAIRS), blk1(2 * N_PAIRS)]
    for d in (4, 16):
        specs += [blkd(d, 0), blkd(d, N_PAIRS), blkd(d, 2 * N_PAIRS)]
    specs.append(pl.BlockSpec((None,) + tbl.shape[1:], lambda p, bi: (p, 0, 0, 0, 0)))
    return pl.pallas_call(
        functools.partial(_dil_kernel, unroll=unroll),
        grid=(N_PAIRS, b),
        in_specs=specs,
        out_specs=pl.BlockSpec((None, None, s, LANES), lambda p, bi: (bi, p, 0, 0)),
        out_shape=jax.ShapeDtypeStruct((b, N_PAIRS, s, LANES), F32),
        scratch_shapes=[pltpu.VMEM((s, LANES), F32)] * 3,
        compiler_params=pltpu.CompilerParams(
            dimension_semantics=("arbitrary", "arbitrary"), vmem_limit_bytes=VMEM_LIMIT),
        name="dil_attn",
    )(d1, d1, d1, d4, d4, d4, d16, d16, d16, tbl)


def _toeplitz(vec, n_rows, n_cols):
    p = n_rows + n_cols - 1
    ext = jnp.concatenate([vec, jnp.zeros(vec.shape[:-1] + (1,), vec.dtype)], axis=-1)
    tiled = jnp.tile(ext, (1,) * (vec.ndim - 1) + (n_rows,))[..., :n_rows * p]
    return tiled.reshape(vec.shape[:-1] + (n_rows, p))[..., n_rows - 1:]


def _pair_rows(tbl):
    lead = tbl.shape[1:-2]
    m, n = tbl.shape[-2:]
    t = tbl.reshape((N_PAIRS, HEADS_PER_TILE) + lead + (m, n))
    t = jnp.moveaxis(t, 1, -3)
    return t.reshape((N_PAIRS,) + lead + (HEADS_PER_TILE * m, n))


def _t5_bucket(rel):
    nb = T5_BUCKETS // 2
    max_exact = nb // 2
    n = jnp.abs(rel)
    large = max_exact + (jnp.log(jnp.maximum(n, 1).astype(F32) / max_exact)
                         / math.log(T5_MAX_DIST / max_exact) * (nb - max_exact)).astype(jnp.int32)
    large = jnp.minimum(large, nb - 1)
    return jnp.where(rel > 0, nb, 0) + jnp.where(n < max_exact, n, large)


def _na_tables(rel_bias):
    pad = GRID_W - NA_KW
    vec = jnp.pad(rel_bias.astype(F32), ((0, 0), (0, 0), (pad, pad)))
    col = _toeplitz(vec, GRID_W, GRID_W)
    qc = np.arange(GRID_W)[:, None]
    kc = np.arange(GRID_W)[None, :]
    qs = np.clip(qc - NA_KW // 2, 0, GRID_W - NA_KW)
    valid = (kc >= qs) & (kc < qs + NA_KW)
    col = jnp.where(valid, col, -jnp.inf)
    variants = []
    for v in range(NA_KH):
        blk = col[:, NA_KH - 1 - v:2 * NA_KH - 1 - v]
        variants.append(jnp.swapaxes(blk, 1, 2).reshape(N_HEADS_NA, GRID_W, NA_KH * GRID_W))
    return _pair_rows(jnp.stack(variants, axis=1))


def _dil_tables(t5_bias):
    span = 2 * DIL_RADIUS + 1
    off = np.arange(span) - DIL_RADIUS
    per_pattern = []
    for _, dil in DIL_PATTERNS:
        bucket = _t5_bucket(jnp.asarray(off * dil, jnp.int32))
        vec = jnp.take(t5_bias.astype(F32), bucket, axis=1)
        per_var = []
        for v in range(3):
            left = DIL_QBLOCK - 1 - DIL_RADIUS + DIL_RADIUS * v
            right = DIL_QBLOCK + DIL_KBLOCK - 1 - span - left
            ext = jnp.pad(vec, ((0, 0), (left, right)), constant_values=-jnp.inf)
            per_var.append(_toeplitz(ext, DIL_QBLOCK, DIL_KBLOCK))
        per_pattern.append(jnp.stack(per_var, axis=1))
    return _pair_rows(jnp.stack(per_pattern, axis=1))


def _group_columns(w_qkv):
    w = w_qkv.reshape(D_MODEL, 3, N_HEADS_NA + N_HEADS_DIL, HEAD_DIM)
    w = w * jnp.asarray([HEAD_DIM ** -0.5, 1.0, 1.0], F32)[None, :, None, None]
    na = w[:, :, :N_HEADS_NA].reshape(D_MODEL, D_GROUP)
    dil = w[:, :, N_HEADS_NA:].reshape(D_MODEL, D_GROUP)
    return jnp.concatenate([na, dil], axis=1).astype(BF16)


def kernel(x, ffn1_pre_g, ffn1_w_gate, ffn1_w_up, ffn1_w_down, ffn1_post_g, mix_pre_g, w_qkv,
           na_rel_bias, t5_rel_bias, na_out_g, dil_out_g, w_out, mix_post_g, ffn2_pre_g,
           ffn2_w_gate, ffn2_w_up, ffn2_w_down, ffn2_post_g):
    b, s, d = x.shape
    depth = ffn1_pre_g.shape[0]
    dil_tbl = _dil_tables(t5_rel_bias)
    ffn_params = lambda pre_g, wg, wu, wd, post_g: (
        pre_g[None], wg.astype(BF16), wu.astype(BF16), wd.astype(BF16), post_g[None])
    for l in range(depth):
        x = _ffn(x, ffn_params(ffn1_pre_g[l], ffn1_w_gate[l], ffn1_w_up[l], ffn1_w_down[l],
                               ffn1_post_g[l]))
        qkv_na, d1, d4, d16 = _qkv(x, mix_pre_g[l][None], _group_columns(w_qkv[l]))
        o_na = _na(qkv_na, _na_tables(na_rel_bias[l]))
        o_dil = _dil(d1, d4, d16, dil_tbl)
        x = _mix_ffn(x, o_na, o_dil, na_out_g[l][None], dil_out_g[l][None],
                     w_out[l].astype(BF16), mix_post_g[l][None],
                     ffn_params(ffn2_pre_g[l], ffn2_w_gate[l], ffn2_w_up[l], ffn2_w_down[l],
                                ffn2_post_g[l]))
    return x
```

```python
import functools
import math

import jax
import jax.numpy as jnp
import numpy as np
from jax import lax
from jax.experimental import pallas as pl
from jax.experimental.pallas import tpu as pltpu

D_MODEL = 1024
HEAD_DIM = 64
N_HEADS_NA = 8
N_HEADS_DIL = 8
D_NA = N_HEADS_NA * HEAD_DIM
D_DIL = N_HEADS_DIL * HEAD_DIM
D_FF = 2816
GRID_W = 64
NA_KH = 8
NA_KW = 16
DIL_PATTERNS = ((128, 1), (512, 4), (2048, 16))
DIL_QBLOCK = 128
DIL_RADIUS = 64
DIL_KBLOCK = DIL_QBLOCK + 2 * DIL_RADIUS
T5_BUCKETS = 32
T5_MAX_DIST = 1024
NORM_EPS = 1e-6

LANES = 128
MXU_COLS = 256
FFN_ROWS = 256
HEADS_PER_TILE = LANES // HEAD_DIM
N_PAIRS = N_HEADS_NA // HEADS_PER_TILE
D_GROUP = 3 * D_NA
VMEM_LIMIT = 56 * 1024 * 1024

BF16 = jnp.bfloat16
F32 = jnp.float32


def _rms(x, g):
    return x * lax.rsqrt(jnp.mean(x * x, axis=-1, keepdims=True) + NORM_EPS) * g


def _resident(shape):
    return pl.BlockSpec(shape, lambda *_: (0,) * len(shape), pipeline_mode=pl.Buffered(1))


def _half_step_ffn(xs, pre_g_ref, wg_ref, wu_ref, wd_ref, post_g_ref):
    hs = [_rms(x, pre_g_ref[...]).astype(BF16) for x in xs]
    gus = [(jnp.dot(h, wg_ref[...], preferred_element_type=F32),
            jnp.dot(h, wu_ref[...], preferred_element_type=F32)) for h in hs]
    acts = [(g * jax.nn.sigmoid(g) * u).astype(BF16) for g, u in gus]
    ys = [jnp.dot(a, wd_ref[...], preferred_element_type=F32) for a in acts]
    return [x + 0.5 * _rms(y, post_g_ref[...]) for x, y in zip(xs, ys)]


def _row_tiles(n_rows):
    return [slice(r, r + FFN_ROWS) for r in range(0, n_rows, FFN_ROWS)]


def _ffn_kernel(x_ref, *refs):
    o_ref = refs[-1]
    tiles = _row_tiles(x_ref.shape[0])
    for t, out in zip(tiles, _half_step_ffn([x_ref[t, :] for t in tiles], *refs[:-1])):
        o_ref[t, :] = out


def _mix_ffn_kernel(x_ref, na_ref, dil_ref, g_na_ref, g_dil_ref, w_ref, g_mix_ref, *refs):
    o_ref = refs[-1]
    tiles = _row_tiles(x_ref.shape[0])
    heads = lambda ref, t: jnp.concatenate([ref[p, t, :] for p in range(N_PAIRS)], axis=-1)
    xs = []
    for t in tiles:
        a = _rms(heads(na_ref, t), g_na_ref[...]).astype(BF16)
        d = _rms(heads(dil_ref, t), g_dil_ref[...]).astype(BF16)
        mixed = (jnp.dot(a, w_ref[:D_NA, :], preferred_element_type=F32)
                 + jnp.dot(d, w_ref[D_NA:, :], preferred_element_type=F32))
        xs.append(x_ref[t, :] + _rms(mixed, g_mix_ref[...]))
    for t, out in zip(tiles, _half_step_ffn(xs, *refs[:-1])):
        o_ref[t, :] = out


def _ffn_specs():
    return [_resident((1, D_MODEL)), _resident((D_MODEL, D_FF)), _resident((D_MODEL, D_FF)),
            _resident((D_FF, D_MODEL)), _resident((1, D_MODEL))]


def _ffn(x, ffn_params, *, tm=512):
    b, s, _ = x.shape
    row = pl.BlockSpec((None, tm, D_MODEL), lambda bi, i: (bi, i, 0))
    return pl.pallas_call(
        _ffn_kernel,
        grid=(b, s // tm),
        in_specs=[row] + _ffn_specs(),
        out_specs=row,
        out_shape=jax.ShapeDtypeStruct((b, s, D_MODEL), F32),
        compiler_params=pltpu.CompilerParams(
            dimension_semantics=("arbitrary", "arbitrary"), vmem_limit_bytes=VMEM_LIMIT),
        name="ffn",
    )(x, *ffn_params)


def _mix_ffn(x, o_na, o_dil, g_na, g_dil, w, g_mix, ffn_params, *, tm=512):
    b, s, _ = x.shape
    row = pl.BlockSpec((None, tm, D_MODEL), lambda bi, i: (bi, i, 0))
    heads = pl.BlockSpec((None, N_PAIRS, tm, LANES), lambda bi, i: (bi, 0, i, 0))
    return pl.pallas_call(
        _mix_ffn_kernel,
        grid=(b, s // tm),
        in_specs=[row, heads, heads, _resident((1, D_NA)), _resident((1, D_DIL)),
                  _resident((D_NA + D_DIL, D_MODEL)), _resident((1, D_MODEL))] + _ffn_specs(),
        out_specs=row,
        out_shape=jax.ShapeDtypeStruct((b, s, D_MODEL), F32),
        compiler_params=pltpu.CompilerParams(
            dimension_semantics=("arbitrary", "arbitrary"), vmem_limit_bytes=VMEM_LIMIT),
        name="mix_ffn",
    )(x, o_na, o_dil, g_na, g_dil, w, g_mix, *ffn_params)


def _qkv_kernel(x_ref, g_ref, w_ref, na_ref, d1_ref, d4_ref, d16_ref, scr_ref, scr4_ref, *, tm):
    h = _rms(x_ref[0], g_ref[...]).astype(BF16)
    for c in range(0, D_GROUP, MXU_COLS):
        y = jnp.dot(h, w_ref[:, D_GROUP + c:D_GROUP + c + MXU_COLS], preferred_element_type=F32)
        for lo in range(0, MXU_COLS, LANES):
            t = (c + lo) // LANES
            y_t = y[:, lo:lo + LANES]
            d1_ref[0, t] = y_t.astype(BF16)
            scr_ref[t] = y_t
            for r in range(4):
                cls = scr_ref[t, pl.ds(r, tm // 4, stride=4), :]
                d4_ref[0, t, r] = cls.astype(BF16)
                scr4_ref[t, r] = cls
                for r2 in range(4):
                    d16_ref[0, t, 4 * r2 + r] = (
                        scr4_ref[t, r, pl.ds(r2, tm // 16, stride=4), :].astype(BF16))
    for c in range(0, D_GROUP, MXU_COLS):
        y = jnp.dot(h, w_ref[:, c:c + MXU_COLS], preferred_element_type=F32)
        for lo in range(0, MXU_COLS, LANES):
            na_ref[0, (c + lo) // LANES] = y[:, lo:lo + LANES].astype(BF16)


def _qkv(x, g, w, *, tm=512):
    b, s, _ = x.shape
    n_t = D_GROUP // LANES
    return pl.pallas_call(
        functools.partial(_qkv_kernel, tm=tm),
        grid=(b, s // tm),
        in_specs=[
            pl.BlockSpec((1, tm, D_MODEL), lambda bi, i: (bi, i, 0)),
            _resident((1, D_MODEL)),
            _resident((D_MODEL, 2 * D_GROUP)),
        ],
        out_specs=[
            pl.BlockSpec((1, n_t, tm, LANES), lambda bi, i: (bi, 0, i, 0)),
            pl.BlockSpec((1, n_t, tm, LANES), lambda bi, i: (bi, 0, i, 0)),
            pl.BlockSpec((1, n_t, 4, tm // 4, LANES), lambda bi, i: (bi, 0, 0, i, 0)),
            pl.BlockSpec((1, n_t, 16, tm // 16, LANES), lambda bi, i: (bi, 0, 0, i, 0)),
        ],
        out_shape=[
            jax.ShapeDtypeStruct((b, n_t, s, LANES), BF16),
            jax.ShapeDtypeStruct((b, n_t, s, LANES), BF16),
            jax.ShapeDtypeStruct((b, n_t, 4, s // 4, LANES), BF16),
            jax.ShapeDtypeStruct((b, n_t, 16, s // 16, LANES), BF16),
        ],
        scratch_shapes=[pltpu.VMEM((n_t, tm, LANES), F32),
                        pltpu.VMEM((n_t, 4, tm // 4, LANES), F32)],
        compiler_params=pltpu.CompilerParams(
            dimension_semantics=("arbitrary", "arbitrary"), vmem_limit_bytes=VMEM_LIMIT),
        name="qkv",
    )(x, g, w)


def _attend_blocks(blocks):
    lane = lax.broadcasted_iota(jnp.int32, (1, LANES), 1)
    second = lane >= HEAD_DIM
    scores = []
    for q, k, _, table in blocks:
        zero = jnp.zeros_like(q)
        q2 = jnp.concatenate([jnp.where(second, zero, q), jnp.where(second, q, zero)], axis=0)
        scores.append(lax.dot_general(q2, k, (((1,), (1,)), ((), ())),
                                      preferred_element_type=F32) + table)
    probs = []
    for s in scores:
        m = jnp.max(s, axis=-1, keepdims=True)
        p = jnp.exp(s - m)
        probs.append((m, jnp.sum(p, axis=-1, keepdims=True), p.astype(BF16)))
    out = []
    for (q, _, v, _), (m, l, p) in zip(blocks, probs):
        m_rows = q.shape[0]
        pv = jnp.dot(p, v, preferred_element_type=F32)
        pick = lambda a, m_rows=m_rows: jnp.where(second, a[m_rows:], a[:m_rows])
        out.append((pick(m), pick(l), pick(pv)))
    return out


def _na_kernel(q_ref, k_ref, v_ref, tbl_ref, o_ref, *, unroll):
    rows = q_ref.shape[0] // GRID_W
    n_keys = NA_KH * GRID_W

    def step(i, carry):
        blocks, starts = [], []
        for u in range(unroll):
            r = i * unroll + u
            kr0 = jnp.clip(r - NA_KH // 2, 0, rows - NA_KH)
            q0 = pl.multiple_of(r * GRID_W, GRID_W)
            k0 = pl.multiple_of(kr0 * GRID_W, GRID_W)
            starts.append(q0)
            blocks.append((q_ref[pl.ds(q0, GRID_W), :], k_ref[pl.ds(k0, n_keys), :],
                           v_ref[pl.ds(k0, n_keys), :], tbl_ref[r - kr0]))
        for q0, (_, l_t, acc_t) in zip(starts, _attend_blocks(blocks)):
            o_ref[pl.ds(q0, GRID_W), :] = acc_t / l_t
        return carry

    lax.fori_loop(0, rows // unroll, step, 0)


def _na(qkv_na, tbl, *, unroll=8):
    b, _, s, _ = qkv_na.shape
    blk = lambda off: pl.BlockSpec((None, None, s, LANES), lambda p, bi: (bi, off + p, 0, 0))
    return pl.pallas_call(
        functools.partial(_na_kernel, unroll=unroll),
        grid=(N_PAIRS, b),
        in_specs=[
            blk(0), blk(N_PAIRS), blk(2 * N_PAIRS),
            pl.BlockSpec((None,) + tbl.shape[1:], lambda p, bi: (p, 0, 0, 0)),
        ],
        out_specs=pl.BlockSpec((None, None, s, LANES), lambda p, bi: (bi, p, 0, 0)),
        out_shape=jax.ShapeDtypeStruct((b, N_PAIRS, s, LANES), F32),
        compiler_params=pltpu.CompilerParams(
            dimension_semantics=("arbitrary", "arbitrary"), vmem_limit_bytes=VMEM_LIMIT),
        name="na_attn",
    )(qkv_na, qkv_na, qkv_na, tbl)


def _dil_kernel(q1_ref, k1_ref, v1_ref, q4_ref, k4_ref, v4_ref, q16_ref, k16_ref, v16_ref,
                tbl_ref, o_ref, m_ref, l_ref, acc_ref, *, unroll):
    s_len = q1_ref.shape[0]
    qb = DIL_QBLOCK
    kb = DIL_KBLOCK

    def window(i, n_blocks):
        q0 = pl.multiple_of(i * qb, qb)
        w0 = pl.multiple_of(jnp.clip(q0 - DIL_RADIUS, 0, n_blocks * qb - kb), DIL_RADIUS)
        var = jnp.where(i == 0, 0, jnp.where(i == n_blocks - 1, 2, 1))
        return q0, w0, var

    def merge(rows, m_b, l_b, acc_b):
        m_old = m_ref[rows, :]
        m_new = jnp.maximum(m_old, m_b)
        a = jnp.exp(m_old - m_new)
        bb = jnp.exp(m_b - m_new)
        return m_new, a * l_ref[rows, :] + bb * l_b, a * acc_ref[rows, :] + bb * acc_b

    def strided_pattern(pat, dil, q_ref, k_ref, v_ref, first):
        nb = s_len // dil // qb

        def body(j, carry):
            blocks, row_sets = [], []
            for u in range(unroll):
                idx = j * unroll + u
                r = idx // nb
                q0, w0, var = window(idx % nb, nb)
                row_sets.append(pl.ds(r + q0 * dil, qb, stride=dil))
                blocks.append((q_ref[r, pl.ds(q0, qb), :], k_ref[r, pl.ds(w0, kb), :],
                               v_ref[r, pl.ds(w0, kb), :], tbl_ref[pat, var]))
            for rows, pieces in zip(row_sets, _attend_blocks(blocks)):
                m_n, l_n, acc_n = pieces if first else merge(rows, *pieces)
                m_ref[rows, :] = m_n
                l_ref[rows, :] = l_n
                acc_ref[rows, :] = acc_n
            return carry

        lax.fori_loop(0, dil * nb // unroll, body, 0)

    strided_pattern(2, 16, q16_ref, k16_ref, v16_ref, True)
    strided_pattern(1, 4, q4_ref, k4_ref, v4_ref, False)

    nb1 = s_len // qb

    def p1(j, carry):
        blocks, starts = [], []
        for u in range(unroll):
            q0, w0, var = window(j * unroll + u, nb1)
            starts.append(q0)
            blocks.append((q1_ref[pl.ds(q0, qb), :], k1_ref[pl.ds(w0, kb), :],
                           v1_ref[pl.ds(w0, kb), :], tbl_ref[0, var]))
        for q0, pieces in zip(starts, _attend_blocks(blocks)):
            _, l_n, acc_n = merge(pl.ds(q0, qb), *pieces)
            o_ref[pl.ds(q0, qb), :] = acc_n / l_n
        return carry

    lax.fori_loop(0, nb1 // unroll, p1, 0)


def _dil(d1, d4, d16, tbl, *, unroll=4):
    b, _, s, _ = d1.shape
    blk1 = lambda off: pl.BlockSpec((None, None, s, LANES), lambda p, bi: (bi, off + p, 0, 0))
    blkd = lambda d, off: pl.BlockSpec((None, None, d, s // d, LANES),
                                       lambda p, bi: (bi, off + p, 0, 0, 0))
    specs = [blk1(0), blk1(N_PAIRS), blk1(2 * N_PAIRS)]
    for d in (4, 16):
        specs += [blkd(d, 0), blkd(d, N_PAIRS), blkd(d, 2 * N_PAIRS)]
    specs.append(pl.BlockSpec((None,) + tbl.shape[1:], lambda p, bi: (p, 0, 0, 0, 0)))
    return pl.pallas_call(
        functools.partial(_dil_kernel, unroll=unroll),
        grid=(N_PAIRS, b),
        in_specs=specs,
        out_specs=pl.BlockSpec((None, None, s, LANES), lambda p, bi: (bi, p, 0, 0)),
        out_shape=jax.ShapeDtypeStruct((b, N_PAIRS, s, LANES), F32),
        scratch_shapes=[pltpu.VMEM((s, LANES), F32)] * 3,
        compiler_params=pltpu.CompilerParams(
            dimension_semantics=("arbitrary", "arbitrary"), vmem_limit_bytes=VMEM_LIMIT),
        name="dil_attn",
    )(d1, d1, d1, d4, d4, d4, d16, d16, d16, tbl)


def _toeplitz(vec, n_rows, n_cols):
    p = n_rows + n_cols - 1
    ext = jnp.concatenate([vec, jnp.zeros(vec.shape[:-1] + (1,), vec.dtype)], axis=-1)
    tiled = jnp.tile(ext, (1,) * (vec.ndim - 1) + (n_rows,))[..., :n_rows * p]
    return tiled.reshape(vec.shape[:-1] + (n_rows, p))[..., n_rows - 1:]


def _pair_rows(tbl):
    lead = tbl.shape[1:-2]
    m, n = tbl.shape[-2:]
    t = tbl.reshape((N_PAIRS, HEADS_PER_TILE) + lead + (m, n))
    t = jnp.moveaxis(t, 1, -3)
    return t.reshape((N_PAIRS,) + lead + (HEADS_PER_TILE * m, n))


def _t5_bucket(rel):
    nb = T5_BUCKETS // 2
    max_exact = nb // 2
    n = jnp.abs(rel)
    large = max_exact + (jnp.log(jnp.maximum(n, 1).astype(F32) / max_exact)
                         / math.log(T5_MAX_DIST / max_exact) * (nb - max_exact)).astype(jnp.int32)
    large = jnp.minimum(large, nb - 1)
    return jnp.where(rel > 0, nb, 0) + jnp.where(n < max_exact, n, large)


def _na_tables(rel_bias):
    pad = GRID_W - NA_KW
    vec = jnp.pad(rel_bias.astype(F32), ((0, 0), (0, 0), (pad, pad)))
    col = _toeplitz(vec, GRID_W, GRID_W)
    qc = np.arange(GRID_W)[:, None]
    kc = np.arange(GRID_W)[None, :]
    qs = np.clip(qc - NA_KW // 2, 0, GRID_W - NA_KW)
    valid = (kc >= qs) & (kc < qs + NA_KW)
    col = jnp.where(valid, col, -jnp.inf)
    variants = []
    for v in range(NA_KH):
        blk = col[:, NA_KH - 1 - v:2 * NA_KH - 1 - v]
        variants.append(jnp.swapaxes(blk, 1, 2).reshape(N_HEADS_NA, GRID_W, NA_KH * GRID_W))
    return _pair_rows(jnp.stack(variants, axis=1))


def _dil_tables(t5_bias):
    span = 2 * DIL_RADIUS + 1
    off = np.arange(span) - DIL_RADIUS
    per_pattern = []
    for _, dil in DIL_PATTERNS:
        bucket = _t5_bucket(jnp.asarray(off * dil, jnp.int32))
        vec = jnp.take(t5_bias.astype(F32), bucket, axis=1)
        per_var = []
        for v in range(3):
            left = DIL_QBLOCK - 1 - DIL_RADIUS + DIL_RADIUS * v
            right = DIL_QBLOCK + DIL_KBLOCK - 1 - span - left
            ext = jnp.pad(vec, ((0, 0), (left, right)), constant_values=-jnp.inf)
            per_var.append(_toeplitz(ext, DIL_QBLOCK, DIL_KBLOCK))
        per_pattern.append(jnp.stack(per_var, axis=1))
    return _pair_rows(jnp.stack(per_pattern, axis=1))


def _group_columns(w_qkv):
    w = w_qkv.reshape(D_MODEL, 3, N_HEADS_NA + N_HEADS_DIL, HEAD_DIM)
    w = w * jnp.asarray([HEAD_DIM ** -0.5, 1.0, 1.0], F32)[None, :, None, None]
    na = w[:, :, :N_HEADS_NA].reshape(D_MODEL, D_GROUP)
    dil = w[:, :, N_HEADS_NA:].reshape(D_MODEL, D_GROUP)
    return jnp.concatenate([na, dil], axis=1).astype(BF16)


def kernel(x, ffn1_pre_g, ffn1_w_gate, ffn1_w_up, ffn1_w_down, ffn1_post_g, mix_pre_g, w_qkv,
           na_rel_bias, t5_rel_bias, na_out_g, dil_out_g, w_out, mix_post_g, ffn2_pre_g,
           ffn2_w_gate, ffn2_w_up, ffn2_w_down, ffn2_post_g):
    b, s, d = x.shape
    depth = ffn1_pre_g.shape[0]
    dil_tbl = _dil_tables(t5_rel_bias)
    ffn_params = lambda pre_g, wg, wu, wd, post_g: (
        pre_g[None], wg.astype(BF16), wu.astype(BF16), wd.astype(BF16), post_g[None])
    for l in range(depth):
        x = _ffn(x, ffn_params(ffn1_pre_g[l], ffn1_w_gate[l], ffn1_w_up[l], ffn1_w_down[l],
                               ffn1_post_g[l]))
        qkv_na, d1, d4, d16 = _qkv(x, mix_pre_g[l][None], _group_columns(w_qkv[l]))
        o_na = _na(qkv_na, _na_tables(na_rel_bias[l]))
        o_dil = _dil(d1, d4, d16, dil_tbl)
        x = _mix_ffn(x, o_na, o_dil, na_out_g[l][None], dil_out_g[l][None],
                     w_out[l].astype(BF16), mix_post_g[l][None],
                     ffn_params(ffn2_pre_g[l], ffn2_w_gate[l], ffn2_w_up[l], ffn2_w_down[l],
                                ffn2_post_g[l]))
    return x
```

```python
import functools
import math

import jax
import jax.numpy as jnp
import numpy as np
from jax import lax
from jax.experimental import pallas as pl
from jax.experimental.pallas import tpu as pltpu

D_MODEL = 1024
HEAD_DIM = 64
N_HEADS_NA = 8
N_HEADS_DIL = 8
D_NA = N_HEADS_NA * HEAD_DIM
D_DIL = N_HEADS_DIL * HEAD_DIM
D_FF = 2816
GRID_W = 64
NA_KH = 8
NA_KW = 16
DIL_PATTERNS = ((128, 1), (512, 4), (2048, 16))
DIL_QBLOCK = 128
DIL_RADIUS = 64
DIL_KBLOCK = DIL_QBLOCK + 2 * DIL_RADIUS
T5_BUCKETS = 32
T5_MAX_DIST = 1024
NORM_EPS = 1e-6

LANES = 128
MXU_COLS = 256
FFN_ROWS = 128
HEADS_PER_TILE = LANES // HEAD_DIM
N_PAIRS = N_HEADS_NA // HEADS_PER_TILE
D_GROUP = 3 * D_NA
VMEM_LIMIT = 56 * 1024 * 1024

BF16 = jnp.bfloat16
F32 = jnp.float32


def _rms(x, g):
    return x * lax.rsqrt(jnp.mean(x * x, axis=-1, keepdims=True) + NORM_EPS) * g


def _resident(shape):
    return pl.BlockSpec(shape, lambda *_: (0,) * len(shape), pipeline_mode=pl.Buffered(1))


def _half_step_ffn(xs, pre_g_ref, wg_ref, wu_ref, wd_ref, post_g_ref):
    hs = [_rms(x, pre_g_ref[...]).astype(BF16) for x in xs]
    gus = [(jnp.dot(h, wg_ref[...], preferred_element_type=F32),
            jnp.dot(h, wu_ref[...], preferred_element_type=F32)) for h in hs]
    acts = [(g * jax.nn.sigmoid(g) * u).astype(BF16) for g, u in gus]
    ys = [jnp.dot(a, wd_ref[...], preferred_element_type=F32) for a in acts]
    return [x + 0.5 * _rms(y, post_g_ref[...]) for x, y in zip(xs, ys)]


def _row_tiles(n_rows):
    return [slice(r, r + FFN_ROWS) for r in range(0, n_rows, FFN_ROWS)]


def _ffn_kernel(x_ref, *refs):
    o_ref = refs[-1]
    tiles = _row_tiles(x_ref.shape[0])
    for t, out in zip(tiles, _half_step_ffn([x_ref[t, :] for t in tiles], *refs[:-1])):
        o_ref[t, :] = out


def _mix_ffn_kernel(x_ref, na_ref, dil_ref, g_na_ref, g_dil_ref, w_ref, g_mix_ref, *refs):
    o_ref = refs[-1]
    tiles = _row_tiles(x_ref.shape[0])
    heads = lambda ref, t: jnp.concatenate([ref[p, t, :] for p in range(N_PAIRS)], axis=-1)
    xs = []
    for t in tiles:
        a = _rms(heads(na_ref, t), g_na_ref[...]).astype(BF16)
        d = _rms(heads(dil_ref, t), g_dil_ref[...]).astype(BF16)
        mixed = (jnp.dot(a, w_ref[:D_NA, :], preferred_element_type=F32)
                 + jnp.dot(d, w_ref[D_NA:, :], preferred_element_type=F32))
        xs.append(x_ref[t, :] + _rms(mixed, g_mix_ref[...]))
    for t, out in zip(tiles, _half_step_ffn(xs, *refs[:-1])):
        o_ref[t, :] = out


def _ffn_specs():
    return [_resident((1, D_MODEL)), _resident((D_MODEL, D_FF)), _resident((D_MODEL, D_FF)),
            _resident((D_FF, D_MODEL)), _resident((1, D_MODEL))]


def _ffn(x, ffn_params, *, tm=512):
    b, s, _ = x.shape
    row = pl.BlockSpec((None, tm, D_MODEL), lambda bi, i: (bi, i, 0))
    return pl.pallas_call(
        _ffn_kernel,
        grid=(b, s // tm),
        in_specs=[row] + _ffn_specs(),
        out_specs=row,
        out_shape=jax.ShapeDtypeStruct((b, s, D_MODEL), F32),
        compiler_params=pltpu.CompilerParams(
            dimension_semantics=("arbitrary", "arbitrary"), vmem_limit_bytes=VMEM_LIMIT),
        name="ffn",
    )(x, *ffn_params)


def _mix_ffn(x, o_na, o_dil, g_na, g_dil, w, g_mix, ffn_params, *, tm=512):
    b, s, _ = x.shape
    row = pl.BlockSpec((None, tm, D_MODEL), lambda bi, i: (bi, i, 0))
    heads = pl.BlockSpec((None, N_PAIRS, tm, LANES), lambda bi, i: (bi, 0, i, 0))
    return pl.pallas_call(
        _mix_ffn_kernel,
        grid=(b, s // tm),
        in_specs=[row, heads, heads, _resident((1, D_NA)), _resident((1, D_DIL)),
                  _resident((D_NA + D_DIL, D_MODEL)), _resident((1, D_MODEL))] + _ffn_specs(),
        out_specs=row,
        out_shape=jax.ShapeDtypeStruct((b, s, D_MODEL), F32),
        compiler_params=pltpu.CompilerParams(
            dimension_semantics=("arbitrary", "arbitrary"), vmem_limit_bytes=VMEM_LIMIT),
        name="mix_ffn",
    )(x, o_na, o_dil, g_na, g_dil, w, g_mix, *ffn_params)


def _qkv_kernel(x_ref, g_ref, w_ref, na_ref, d1_ref, d4_ref, d16_ref, scr_ref, scr4_ref, *, tm):
    h = _rms(x_ref[0], g_ref[...]).astype(BF16)
    for c in range(0, D_GROUP, MXU_COLS):
        y = jnp.dot(h, w_ref[:, D_GROUP + c:D_GROUP + c + MXU_COLS], preferred_element_type=F32)
        for lo in range(0, MXU_COLS, LANES):
            t = (c + lo) // LANES
            y_t = y[:, lo:lo + LANES]
            d1_ref[0, t] = y_t.astype(BF16)
            scr_ref[t] = y_t
            for r in range(4):
                cls = scr_ref[t, pl.ds(r, tm // 4, stride=4), :]
                d4_ref[0, t, r] = cls.astype(BF16)
                scr4_ref[t, r] = cls
                for r2 in range(4):
                    d16_ref[0, t, 4 * r2 + r] = (
                        scr4_ref[t, r, pl.ds(r2, tm // 16, stride=4), :].astype(BF16))
    for c in range(0, D_GROUP, MXU_COLS):
        y = jnp.dot(h, w_ref[:, c:c + MXU_COLS], preferred_element_type=F32)
        for lo in range(0, MXU_COLS, LANES):
            na_ref[0, (c + lo) // LANES] = y[:, lo:lo + LANES].astype(BF16)


def _qkv(x, g, w, *, tm=512):
    b, s, _ = x.shape
    n_t = D_GROUP // LANES
    return pl.pallas_call(
        functools.partial(_qkv_kernel, tm=tm),
        grid=(b, s // tm),
        in_specs=[
            pl.BlockSpec((1, tm, D_MODEL), lambda bi, i: (bi, i, 0)),
            _resident((1, D_MODEL)),
            _resident((D_MODEL, 2 * D_GROUP)),
        ],
        out_specs=[
            pl.BlockSpec((1, n_t, tm, LANES), lambda bi, i: (bi, 0, i, 0)),
            pl.BlockSpec((1, n_t, tm, LANES), lambda bi, i: (bi, 0, i, 0)),
            pl.BlockSpec((1, n_t, 4, tm // 4, LANES), lambda bi, i: (bi, 0, 0, i, 0)),
            pl.BlockSpec((1, n_t, 16, tm // 16, LANES), lambda bi, i: (bi, 0, 0, i, 0)),
        ],
        out_shape=[
            jax.ShapeDtypeStruct((b, n_t, s, LANES), BF16),
            jax.ShapeDtypeStruct((b, n_t, s, LANES), BF16),
            jax.ShapeDtypeStruct((b, n_t, 4, s // 4, LANES), BF16),
            jax.ShapeDtypeStruct((b, n_t, 16, s // 16, LANES), BF16),
        ],
        scratch_shapes=[pltpu.VMEM((n_t, tm, LANES), F32),
                        pltpu.VMEM((n_t, 4, tm // 4, LANES), F32)],
        compiler_params=pltpu.CompilerParams(
            dimension_semantics=("arbitrary", "arbitrary"), vmem_limit_bytes=VMEM_LIMIT),
        name="qkv",
    )(x, g, w)


def _attend_blocks(blocks):
    lane = lax.broadcasted_iota(jnp.int32, (1, LANES), 1)
    second = lane >= HEAD_DIM
    scores = []
    for q, k, _, table in blocks:
        zero = jnp.zeros_like(q)
        q2 = jnp.concatenate([jnp.where(second, zero, q), jnp.where(second, q, zero)], axis=0)
        scores.append(lax.dot_general(q2, k, (((1,), (1,)), ((), ())),
                                      preferred_element_type=F32) + table)
    probs = []
    for s in scores:
        m = jnp.max(s, axis=-1, keepdims=True)
        p = jnp.exp(s - m)
        probs.append((m, jnp.sum(p, axis=-1, keepdims=True), p.astype(BF16)))
    out = []
    for (q, _, v, _), (m, l, p) in zip(blocks, probs):
        m_rows = q.shape[0]
        pv = jnp.dot(p, v, preferred_element_type=F32)
        pick = lambda a, m_rows=m_rows: jnp.where(second, a[m_rows:], a[:m_rows])
        out.append((pick(m), pick(l), pick(pv)))
    return out


def _na_kernel(q_ref, k_ref, v_ref, tbl_ref, o_ref, *, unroll):
    rows = q_ref.shape[0] // GRID_W
    n_keys = NA_KH * GRID_W

    def step(i, carry):
        blocks, starts = [], []
        for u in range(unroll):
            r = i * unroll + u
            kr0 = jnp.clip(r - NA_KH // 2, 0, rows - NA_KH)
            q0 = pl.multiple_of(r * GRID_W, GRID_W)
            k0 = pl.multiple_of(kr0 * GRID_W, GRID_W)
            starts.append(q0)
            blocks.append((q_ref[pl.ds(q0, GRID_W), :], k_ref[pl.ds(k0, n_keys), :],
                           v_ref[pl.ds(k0, n_keys), :], tbl_ref[r - kr0]))
        for q0, (_, l_t, acc_t) in zip(starts, _attend_blocks(blocks)):
            o_ref[pl.ds(q0, GRID_W), :] = acc_t / l_t
        return carry

    lax.fori_loop(0, rows // unroll, step, 0)


def _na(qkv_na, tbl, *, unroll=8):
    b, _, s, _ = qkv_na.shape
    blk = lambda off: pl.BlockSpec((None, None, s, LANES), lambda p, bi: (bi, off + p, 0, 0))
    return pl.pallas_call(
        functools.partial(_na_kernel, unroll=unroll),
        grid=(N_PAIRS, b),
        in_specs=[
            blk(0), blk(N_PAIRS), blk(2 * N_PAIRS),
            pl.BlockSpec((None,) + tbl.shape[1:], lambda p, bi: (p, 0, 0, 0)),
        ],
        out_specs=pl.BlockSpec((None, None, s, LANES), lambda p, bi: (bi, p, 0, 0)),
        out_shape=jax.ShapeDtypeStruct((b, N_PAIRS, s, LANES), F32),
        compiler_params=pltpu.CompilerParams(
            dimension_semantics=("arbitrary", "arbitrary"), vmem_limit_bytes=VMEM_LIMIT),
        name="na_attn",
    )(qkv_na, qkv_na, qkv_na, tbl)


def _dil_kernel(q1_ref, k1_ref, v1_ref, q4_ref, k4_ref, v4_ref, q16_ref, k16_ref, v16_ref,
                tbl_ref, o_ref, m_ref, l_ref, acc_ref, *, unroll):
    s_len = q1_ref.shape[0]
    qb = DIL_QBLOCK
    kb = DIL_KBLOCK

    def window(i, n_blocks):
        q0 = pl.multiple_of(i * qb, qb)
        w0 = pl.multiple_of(jnp.clip(q0 - DIL_RADIUS, 0, n_blocks * qb - kb), DIL_RADIUS)
        var = jnp.where(i == 0, 0, jnp.where(i == n_blocks - 1, 2, 1))
        return q0, w0, var

    def merge(rows, m_b, l_b, acc_b):
        m_old = m_ref[rows, :]
        m_new = jnp.maximum(m_old, m_b)
        a = jnp.exp(m_old - m_new)
        bb = jnp.exp(m_b - m_new)
        return m_new, a * l_ref[rows, :] + bb * l_b, a * acc_ref[rows, :] + bb * acc_b

    def strided_pattern(pat, dil, q_ref, k_ref, v_ref, first):
        nb = s_len // dil // qb

        def body(j, carry):
            blocks, row_sets = [], []
            for u in range(unroll):
                idx = j * unroll + u
                r = idx // nb
                q0, w0, var = window(idx % nb, nb)
                row_sets.append(pl.ds(r + q0 * dil, qb, stride=dil))
                blocks.append((q_ref[r, pl.ds(q0, qb), :], k_ref[r, pl.ds(w0, kb), :],
                               v_ref[r, pl.ds(w0, kb), :], tbl_ref[pat, var]))
            for rows, pieces in zip(row_sets, _attend_blocks(blocks)):
                m_n, l_n, acc_n = pieces if first else merge(rows, *pieces)
                m_ref[rows, :] = m_n
                l_ref[rows, :] = l_n
                acc_ref[rows, :] = acc_n
            return carry

        lax.fori_loop(0, dil * nb // unroll, body, 0)

    strided_pattern(2, 16, q16_ref, k16_ref, v16_ref, True)
    strided_pattern(1, 4, q4_ref, k4_ref, v4_ref, False)

    nb1 = s_len // qb

    def p1(j, carry):
        blocks, starts = [], []
        for u in range(unroll):
            q0, w0, var = window(j * unroll + u, nb1)
            starts.append(q0)
            blocks.append((q1_ref[pl.ds(q0, qb), :], k1_ref[pl.ds(w0, kb), :],
                           v1_ref[pl.ds(w0, kb), :], tbl_ref[0, var]))
        for q0, pieces in zip(starts, _attend_blocks(blocks)):
            _, l_n, acc_n = merge(pl.ds(q0, qb), *pieces)
            o_ref[pl.ds(q0, qb), :] = acc_n / l_n
        return carry

    lax.fori_loop(0, nb1 // unroll, p1, 0)


def _dil(d1, d4, d16, tbl, *, unroll=4):
    b, _, s, _ = d1.shape
    blk1 = lambda off: pl.BlockSpec((None, None, s, LANES), lambda p, bi: (bi, off + p, 0, 0))
    blkd = lambda d, off: pl.BlockSpec((None, None, d, s // d, LANES),
                                       lambda p, bi: (bi, off + p, 0, 0, 0))
    specs = [blk1(0), blk1(N_PAIRS), blk1(2 * N_PAIRS)]
    for d in (4, 16):
        specs += [blkd(d, 0), blkd(d, N_PAIRS), blkd(d, 2 * N_PAIRS)]
    specs.append(pl.BlockSpec((None,) + tbl.shape[1:], lambda p, bi: (p, 0, 0, 0, 0)))
    return pl.pallas_call(
        functools.partial(_dil_kernel, unroll=unroll),
        grid=(N_PAIRS, b),
        in_specs=specs,
        out_specs=pl.BlockSpec((None, None, s, LANES), lambda p, bi: (bi, p, 0, 0)),
        out_shape=jax.ShapeDtypeStruct((b, N_PAIRS, s, LANES), F32),
        scratch_shapes=[pltpu.VMEM((s, LANES), F32)] * 3,
        compiler_params=pltpu.CompilerParams(
            dimension_semantics=("arbitrary", "arbitrary"), vmem_limit_bytes=VMEM_LIMIT),
        name="dil_attn",
    )(d1, d1, d1, d4, d4, d4, d16, d16, d16, tbl)


def _toeplitz(vec, n_rows, n_cols):
    p = n_rows + n_cols - 1
    ext = jnp.concatenate([vec, jnp.zeros(vec.shape[:-1] + (1,), vec.dtype)], axis=-1)
    tiled = jnp.tile(ext, (1,) * (vec.ndim - 1) + (n_rows,))[..., :n_rows * p]
    return tiled.reshape(vec.shape[:-1] + (n_rows, p))[..., n_rows - 1:]


def _pair_rows(tbl):
    lead = tbl.shape[1:-2]
    m, n = tbl.shape[-2:]
    t = tbl.reshape((N_PAIRS, HEADS_PER_TILE) + lead + (m, n))
    t = jnp.moveaxis(t, 1, -3)
    return t.reshape((N_PAIRS,) + lead + (HEADS_PER_TILE * m, n))


def _t5_bucket(rel):
    nb = T5_BUCKETS // 2
    max_exact = nb // 2
    n = jnp.abs(rel)
    large = max_exact + (jnp.log(jnp.maximum(n, 1).astype(F32) / max_exact)
                         / math.log(T5_MAX_DIST / max_exact) * (nb - max_exact)).astype(jnp.int32)
    large = jnp.minimum(large, nb - 1)
    return jnp.where(rel > 0, nb, 0) + jnp.where(n < max_exact, n, large)


def _na_tables(rel_bias):
    pad = GRID_W - NA_KW
    vec = jnp.pad(rel_bias.astype(F32), ((0, 0), (0, 0), (pad, pad)))
    col = _toeplitz(vec, GRID_W, GRID_W)
    qc = np.arange(GRID_W)[:, None]
    kc = np.arange(GRID_W)[None, :]
    qs = np.clip(qc - NA_KW // 2, 0, GRID_W - NA_KW)
    valid = (kc >= qs) & (kc < qs + NA_KW)
    col = jnp.where(valid, col, -jnp.inf)
    rows = _toeplitz(jnp.moveaxis(col, 1, -1), NA_KH, NA_KH)
    tbl = jnp.transpose(rows, (0, 3, 1, 4, 2))
    return _pair_rows(tbl.reshape(N_HEADS_NA, NA_KH, GRID_W, NA_KH * GRID_W))


def _dil_tables(t5_bias):
    span = 2 * DIL_RADIUS + 1
    off = np.arange(span) - DIL_RADIUS
    dils = np.asarray([d for _, d in DIL_PATTERNS])
    bucket = _t5_bucket(jnp.asarray(off[None, :] * dils[:, None], jnp.int32))
    vec = jnp.take(t5_bias.astype(F32), bucket, axis=1)
    n_ext = DIL_QBLOCK + DIL_KBLOCK - 1
    left = DIL_QBLOCK - 1 - DIL_RADIUS + 2 * DIL_RADIUS
    padded = jnp.pad(vec, ((0, 0), (0, 0), (left, n_ext)), constant_values=-jnp.inf)
    ext = jnp.stack([padded[..., DIL_RADIUS * (2 - v):DIL_RADIUS * (2 - v) + n_ext]
                     for v in range(3)], axis=2)
    return _pair_rows(_toeplitz(ext, DIL_QBLOCK, DIL_KBLOCK))


def _group_columns(w_qkv):
    w = w_qkv.reshape(D_MODEL, 3, N_HEADS_NA + N_HEADS_DIL, HEAD_DIM)
    w = w * jnp.asarray([HEAD_DIM ** -0.5, 1.0, 1.0], F32)[None, :, None, None]
    na = w[:, :, :N_HEADS_NA].reshape(D_MODEL, D_GROUP)
    dil = w[:, :, N_HEADS_NA:].reshape(D_MODEL, D_GROUP)
    return jnp.concatenate([na, dil], axis=1).astype(BF16)


def kernel(x, ffn1_pre_g, ffn1_w_gate, ffn1_w_up, ffn1_w_down, ffn1_post_g, mix_pre_g, w_qkv,
           na_rel_bias, t5_rel_bias, na_out_g, dil_out_g, w_out, mix_post_g, ffn2_pre_g,
           ffn2_w_gate, ffn2_w_up, ffn2_w_down, ffn2_post_g):
    b, s, d = x.shape
    depth = ffn1_pre_g.shape[0]
    dil_tbl = _dil_tables(t5_rel_bias)
    ffn_params = lambda pre_g, wg, wu, wd, post_g: (
        pre_g[None], wg.astype(BF16), wu.astype(BF16), wd.astype(BF16), post_g[None])
    for l in range(depth):
        x = _ffn(x, ffn_params(ffn1_pre_g[l], ffn1_w_gate[l], ffn1_w_up[l], ffn1_w_down[l],
                               ffn1_post_g[l]))
        qkv_na, d1, d4, d16 = _qkv(x, mix_pre_g[l][None], _group_columns(w_qkv[l]))
        o_na = _na(qkv_na, _na_tables(na_rel_bias[l]))
        o_dil = _dil(d1, d4, d16, dil_tbl)
        x = _mix_ffn(x, o_na, o_dil, na_out_g[l][None], dil_out_g[l][None],
                     w_out[l].astype(BF16), mix_post_g[l][None],
                     ffn_params(ffn2_pre_g[l], ffn2_w_gate[l], ffn2_w_up[l], ffn2_w_down[l],
                                ffn2_post_g[l]))
    return x
```

```python
import functools
import math

import jax
import jax.numpy as jnp
import numpy as np
from jax import lax
from jax.experimental import pallas as pl
from jax.experimental.pallas import tpu as pltpu

D_MODEL = 1024
HEAD_DIM = 64
N_HEADS_NA = 8
N_HEADS_DIL = 8
D_NA = N_HEADS_NA * HEAD_DIM
D_DIL = N_HEADS_DIL * HEAD_DIM
D_FF = 2816
GRID_W = 64
NA_KH = 8
NA_KW = 16
DIL_PATTERNS = ((128, 1), (512, 4), (2048, 16))
DIL_QBLOCK = 128
DIL_RADIUS = 64
DIL_KBLOCK = DIL_QBLOCK + 2 * DIL_RADIUS
T5_BUCKETS = 32
T5_MAX_DIST = 1024
NORM_EPS = 1e-6

LANES = 128
MXU_COLS = 256
FFN_ROWS = 128
HEADS_PER_TILE = LANES // HEAD_DIM
N_PAIRS = N_HEADS_NA // HEADS_PER_TILE
D_GROUP = 3 * D_NA
VMEM_LIMIT = 56 * 1024 * 1024

BF16 = jnp.bfloat16
F32 = jnp.float32


def _rms(x, g):
    return x * lax.rsqrt(jnp.mean(x * x, axis=-1, keepdims=True) + NORM_EPS) * g


def _resident(shape):
    return pl.BlockSpec(shape, lambda *_: (0,) * len(shape), pipeline_mode=pl.Buffered(1))


def _half_step_ffn(xs, pre_g_ref, wg_ref, wu_ref, wd_ref, post_g_ref):
    hs = [_rms(x, pre_g_ref[...]).astype(BF16) for x in xs]
    gus = [(jnp.dot(h, wg_ref[...], preferred_element_type=F32),
            jnp.dot(h, wu_ref[...], preferred_element_type=F32)) for h in hs]
    acts = [(g * jax.nn.sigmoid(g) * u).astype(BF16) for g, u in gus]
    ys = [jnp.dot(a, wd_ref[...], preferred_element_type=F32) for a in acts]
    return [x + 0.5 * _rms(y, post_g_ref[...]) for x, y in zip(xs, ys)]


def _row_tiles(n_rows):
    return [slice(r, r + FFN_ROWS) for r in range(0, n_rows, FFN_ROWS)]


def _ffn_kernel(x_ref, *refs):
    o_ref = refs[-1]
    tiles = _row_tiles(x_ref.shape[0])
    for t, out in zip(tiles, _half_step_ffn([x_ref[t, :] for t in tiles], *refs[:-1])):
        o_ref[t, :] = out


def _mix_ffn_kernel(x_ref, na_ref, dil_ref, g_na_ref, g_dil_ref, w_ref, g_mix_ref, *refs):
    o_ref = refs[-1]
    tiles = _row_tiles(x_ref.shape[0])
    heads = lambda ref, t: jnp.concatenate([ref[p, t, :] for p in range(N_PAIRS)], axis=-1)
    xs = []
    for t in tiles:
        a = _rms(heads(na_ref, t), g_na_ref[...]).astype(BF16)
        d = _rms(heads(dil_ref, t), g_dil_ref[...]).astype(BF16)
        mixed = (jnp.dot(a, w_ref[:D_NA, :], preferred_element_type=F32)
                 + jnp.dot(d, w_ref[D_NA:, :], preferred_element_type=F32))
        xs.append(x_ref[t, :] + _rms(mixed, g_mix_ref[...]))
    for t, out in zip(tiles, _half_step_ffn(xs, *refs[:-1])):
        o_ref[t, :] = out


def _ffn_specs():
    return [_resident((1, D_MODEL)), _resident((D_MODEL, D_FF)), _resident((D_MODEL, D_FF)),
            _resident((D_FF, D_MODEL)), _resident((1, D_MODEL))]


def _ffn(x, ffn_params, *, tm=512):
    b, s, _ = x.shape
    row = pl.BlockSpec((None, tm, D_MODEL), lambda bi, i: (bi, i, 0))
    return pl.pallas_call(
        _ffn_kernel,
        grid=(b, s // tm),
        in_specs=[row] + _ffn_specs(),
        out_specs=row,
        out_shape=jax.ShapeDtypeStruct((b, s, D_MODEL), F32),
        compiler_params=pltpu.CompilerParams(
            dimension_semantics=("arbitrary", "arbitrary"), vmem_limit_bytes=VMEM_LIMIT),
        name="ffn",
    )(x, *ffn_params)


def _mix_ffn(x, o_na, o_dil, g_na, g_dil, w, g_mix, ffn_params, *, tm=512):
    b, s, _ = x.shape
    row = pl.BlockSpec((None, tm, D_MODEL), lambda bi, i: (bi, i, 0))
    heads = pl.BlockSpec((None, N_PAIRS, tm, LANES), lambda bi, i: (bi, 0, i, 0))
    return pl.pallas_call(
        _mix_ffn_kernel,
        grid=(b, s // tm),
        in_specs=[row, heads, heads, _resident((1, D_NA)), _resident((1, D_DIL)),
                  _resident((D_NA + D_DIL, D_MODEL)), _resident((1, D_MODEL))] + _ffn_specs(),
        out_specs=row,
        out_shape=jax.ShapeDtypeStruct((b, s, D_MODEL), F32),
        compiler_params=pltpu.CompilerParams(
            dimension_semantics=("arbitrary", "arbitrary"), vmem_limit_bytes=VMEM_LIMIT),
        name="mix_ffn",
    )(x, o_na, o_dil, g_na, g_dil, w, g_mix, *ffn_params)


def _qkv_kernel(x_ref, g_ref, w_ref, na_ref, d1_ref, d4_ref, d16_ref, scr_ref, scr4_ref, *, tm):
    h = _rms(x_ref[0], g_ref[...]).astype(BF16)
    for c in range(0, D_GROUP, MXU_COLS):
        y = jnp.dot(h, w_ref[:, D_GROUP + c:D_GROUP + c + MXU_COLS], preferred_element_type=F32)
        for lo in range(0, MXU_COLS, LANES):
            t = (c + lo) // LANES
            y_t = y[:, lo:lo + LANES]
            d1_ref[0, t] = y_t.astype(BF16)
            scr_ref[t] = y_t
            for r in range(4):
                cls = scr_ref[t, pl.ds(r, tm // 4, stride=4), :]
                d4_ref[0, t, r] = cls.astype(BF16)
                scr4_ref[t, r] = cls
                for r2 in range(4):
                    d16_ref[0, t, 4 * r2 + r] = (
                        scr4_ref[t, r, pl.ds(r2, tm // 16, stride=4), :].astype(BF16))
    for c in range(0, D_GROUP, MXU_COLS):
        y = jnp.dot(h, w_ref[:, c:c + MXU_COLS], preferred_element_type=F32)
        for lo in range(0, MXU_COLS, LANES):
            na_ref[0, (c + lo) // LANES] = y[:, lo:lo + LANES].astype(BF16)


def _qkv(x, g, w, *, tm=512):
    b, s, _ = x.shape
    n_t = D_GROUP // LANES
    return pl.pallas_call(
        functools.partial(_qkv_kernel, tm=tm),
        grid=(b, s // tm),
        in_specs=[
            pl.BlockSpec((1, tm, D_MODEL), lambda bi, i: (bi, i, 0)),
            _resident((1, D_MODEL)),
            _resident((D_MODEL, 2 * D_GROUP)),
        ],
        out_specs=[
            pl.BlockSpec((1, n_t, tm, LANES), lambda bi, i: (bi, 0, i, 0)),
            pl.BlockSpec((1, n_t, tm, LANES), lambda bi, i: (bi, 0, i, 0)),
            pl.BlockSpec((1, n_t, 4, tm // 4, LANES), lambda bi, i: (bi, 0, 0, i, 0)),
            pl.BlockSpec((1, n_t, 16, tm // 16, LANES), lambda bi, i: (bi, 0, 0, i, 0)),
        ],
        out_shape=[
            jax.ShapeDtypeStruct((b, n_t, s, LANES), BF16),
            jax.ShapeDtypeStruct((b, n_t, s, LANES), BF16),
            jax.ShapeDtypeStruct((b, n_t, 4, s // 4, LANES), BF16),
            jax.ShapeDtypeStruct((b, n_t, 16, s // 16, LANES), BF16),
        ],
        scratch_shapes=[pltpu.VMEM((n_t, tm, LANES), F32),
                        pltpu.VMEM((n_t, 4, tm // 4, LANES), F32)],
        compiler_params=pltpu.CompilerParams(
            dimension_semantics=("arbitrary", "arbitrary"), vmem_limit_bytes=VMEM_LIMIT),
        name="qkv",
    )(x, g, w)


def _attend_blocks(blocks):
    lane = lax.broadcasted_iota(jnp.int32, (1, LANES), 1)
    second = lane >= HEAD_DIM
    scores = []
    for q, k, _, table in blocks:
        zero = jnp.zeros_like(q)
        q2 = jnp.concatenate([jnp.where(second, zero, q), jnp.where(second, q, zero)], axis=0)
        scores.append(lax.dot_general(q2, k, (((1,), (1,)), ((), ())),
                                      preferred_element_type=F32) + table)
    probs = []
    for s in scores:
        m = jnp.max(s, axis=-1, keepdims=True)
        p = jnp.exp(s - m)
        probs.append((m, jnp.sum(p, axis=-1, keepdims=True), p.astype(BF16)))
    out = []
    for (q, _, v, _), (m, l, p) in zip(blocks, probs):
        m_rows = q.shape[0]
        pv = jnp.dot(p, v, preferred_element_type=F32)
        pick = lambda a, m_rows=m_rows: jnp.where(second, a[m_rows:], a[:m_rows])
        out.append((pick(m), pick(l), pick(pv)))
    return out


def _toeplitz_rows(base_row, n_rows):
    base = jnp.broadcast_to(base_row, (n_rows, base_row.shape[-1]))
    return pltpu.roll(base, 0, 1, stride=1, stride_axis=0)


def _na_kernel(q_ref, k_ref, v_ref, base_ref, o_ref, tbl_ref, *, unroll):
    rows = q_ref.shape[0] // GRID_W
    n_keys = NA_KH * GRID_W

    @pl.when(pl.program_id(1) == 0)
    def _():
        qc = lax.broadcasted_iota(jnp.int32, (GRID_W, LANES), 0)
        kc = lax.broadcasted_iota(jnp.int32, (GRID_W, LANES), 1) % GRID_W
        qs = jnp.clip(qc - NA_KW // 2, 0, GRID_W - NA_KW)
        valid = (kc >= qs) & (kc < qs + NA_KW)
        for h in range(HEADS_PER_TILE):
            for dr in range(2 * NA_KH - 2):
                tile = jnp.where(valid, _toeplitz_rows(base_ref[dr, h:h + 1, :], GRID_W), -jnp.inf)
                for t in range(n_keys // LANES):
                    v = NA_KH - 1 - dr + 2 * t
                    if 0 <= v < NA_KH:
                        tbl_ref[v, h * GRID_W:(h + 1) * GRID_W, t * LANES:(t + 1) * LANES] = tile

    def step(i, carry):
        blocks, starts = [], []
        for u in range(unroll):
            r = i * unroll + u
            kr0 = jnp.clip(r - NA_KH // 2, 0, rows - NA_KH)
            q0 = pl.multiple_of(r * GRID_W, GRID_W)
            k0 = pl.multiple_of(kr0 * GRID_W, GRID_W)
            starts.append(q0)
            blocks.append((q_ref[pl.ds(q0, GRID_W), :], k_ref[pl.ds(k0, n_keys), :],
                           v_ref[pl.ds(k0, n_keys), :], tbl_ref[r - kr0]))
        for q0, (_, l_t, acc_t) in zip(starts, _attend_blocks(blocks)):
            o_ref[pl.ds(q0, GRID_W), :] = acc_t / l_t
        return carry

    lax.fori_loop(0, rows // unroll, step, 0)


def _na(qkv_na, bases, *, unroll=8):
    b, _, s, _ = qkv_na.shape
    blk = lambda off: pl.BlockSpec((None, None, s, LANES), lambda p, bi: (bi, off + p, 0, 0))
    return pl.pallas_call(
        functools.partial(_na_kernel, unroll=unroll),
        grid=(N_PAIRS, b),
        in_specs=[
            blk(0), blk(N_PAIRS), blk(2 * N_PAIRS),
            pl.BlockSpec((None,) + bases.shape[1:], lambda p, bi: (p, 0, 0, 0)),
        ],
        out_specs=pl.BlockSpec((None, None, s, LANES), lambda p, bi: (bi, p, 0, 0)),
        out_shape=jax.ShapeDtypeStruct((b, N_PAIRS, s, LANES), F32),
        scratch_shapes=[pltpu.VMEM((NA_KH, HEADS_PER_TILE * GRID_W, NA_KH * GRID_W), F32)],
        compiler_params=pltpu.CompilerParams(
            dimension_semantics=("arbitrary", "arbitrary"), vmem_limit_bytes=VMEM_LIMIT),
        name="na_attn",
    )(qkv_na, qkv_na, qkv_na, bases)


def _dil_kernel(q1_ref, k1_ref, v1_ref, q4_ref, k4_ref, v4_ref, q16_ref, k16_ref, v16_ref,
                base_ref, o_ref, m_ref, l_ref, acc_ref, tbl_ref, *, unroll):
    s_len = q1_ref.shape[0]
    qb = DIL_QBLOCK
    kb = DIL_KBLOCK

    @pl.when(pl.program_id(1) == 0)
    def _():
        for pat in range(len(DIL_PATTERNS)):
            for var in range(3):
                for h in range(HEADS_PER_TILE):
                    tbl_ref[pat, var, h * qb:(h + 1) * qb, :] = (
                        _toeplitz_rows(base_ref[pat, var, h:h + 1, :], qb)[:, :kb])

    def window(i, n_blocks):
        q0 = pl.multiple_of(i * qb, qb)
        w0 = pl.multiple_of(jnp.clip(q0 - DIL_RADIUS, 0, n_blocks * qb - kb), DIL_RADIUS)
        var = jnp.where(i == 0, 0, jnp.where(i == n_blocks - 1, 2, 1))
        return q0, w0, var

    def merge(rows, m_b, l_b, acc_b):
        m_old = m_ref[rows, :]
        m_new = jnp.maximum(m_old, m_b)
        a = jnp.exp(m_old - m_new)
        bb = jnp.exp(m_b - m_new)
        return m_new, a * l_ref[rows, :] + bb * l_b, a * acc_ref[rows, :] + bb * acc_b

    def strided_pattern(pat, dil, q_ref, k_ref, v_ref, first):
        nb = s_len // dil // qb

        def body(j, carry):
            blocks, row_sets = [], []
            for u in range(unroll):
                idx = j * unroll + u
                r = idx // nb
                q0, w0, var = window(idx % nb, nb)
                row_sets.append(pl.ds(r + q0 * dil, qb, stride=dil))
                blocks.append((q_ref[r, pl.ds(q0, qb), :], k_ref[r, pl.ds(w0, kb), :],
                               v_ref[r, pl.ds(w0, kb), :], tbl_ref[pat, var]))
            for rows, pieces in zip(row_sets, _attend_blocks(blocks)):
                m_n, l_n, acc_n = pieces if first else merge(rows, *pieces)
                m_ref[rows, :] = m_n
                l_ref[rows, :] = l_n
                acc_ref[rows, :] = acc_n
            return carry

        lax.fori_loop(0, dil * nb // unroll, body, 0)

    strided_pattern(2, 16, q16_ref, k16_ref, v16_ref, True)
    strided_pattern(1, 4, q4_ref, k4_ref, v4_ref, False)

    nb1 = s_len // qb

    def p1(j, carry):
        blocks, starts = [], []
        for u in range(unroll):
            q0, w0, var = window(j * unroll + u, nb1)
            starts.append(q0)
            blocks.append((q1_ref[pl.ds(q0, qb), :], k1_ref[pl.ds(w0, kb), :],
                           v1_ref[pl.ds(w0, kb), :], tbl_ref[0, var]))
        for q0, pieces in zip(starts, _attend_blocks(blocks)):
            _, l_n, acc_n = merge(pl.ds(q0, qb), *pieces)
            o_ref[pl.ds(q0, qb), :] = acc_n / l_n
        return carry

    lax.fori_loop(0, nb1 // unroll, p1, 0)


def _dil(d1, d4, d16, bases, *, unroll=4):
    b, _, s, _ = d1.shape
    blk1 = lambda off: pl.BlockSpec((None, None, s, LANES), lambda p, bi: (bi, off + p, 0, 0))
    blkd = lambda d, off: pl.BlockSpec((None, None, d, s // d, LANES),
                                       lambda p, bi: (bi, off + p, 0, 0, 0))
    specs = [blk1(0), blk1(N_PAIRS), blk1(2 * N_PAIRS)]
    for d in (4, 16):
        specs += [blkd(d, 0), blkd(d, N_PAIRS), blkd(d, 2 * N_PAIRS)]
    specs.append(pl.BlockSpec((None,) + bases.shape[1:], lambda p, bi: (p, 0, 0, 0, 0)))
    n_pat, n_var = bases.shape[1:3]
    return pl.pallas_call(
        functools.partial(_dil_kernel, unroll=unroll),
        grid=(N_PAIRS, b),
        in_specs=specs,
        out_specs=pl.BlockSpec((None, None, s, LANES), lambda p, bi: (bi, p, 0, 0)),
        out_shape=jax.ShapeDtypeStruct((b, N_PAIRS, s, LANES), F32),
        scratch_shapes=[pltpu.VMEM((s, LANES), F32)] * 3
        + [pltpu.VMEM((n_pat, n_var, HEADS_PER_TILE * DIL_QBLOCK, DIL_KBLOCK), F32)],
        compiler_params=pltpu.CompilerParams(
            dimension_semantics=("arbitrary", "arbitrary"), vmem_limit_bytes=VMEM_LIMIT),
        name="dil_attn",
    )(d1, d1, d1, d4, d4, d4, d16, d16, d16, bases)


def _by_pair(bases):
    t = bases.reshape((N_PAIRS, HEADS_PER_TILE) + bases.shape[1:])
    return jnp.moveaxis(t, 1, -2)


def _t5_bucket(rel):
    nb = T5_BUCKETS // 2
    max_exact = nb // 2
    n = jnp.abs(rel)
    large = max_exact + (jnp.log(jnp.maximum(n, 1).astype(F32) / max_exact)
                         / math.log(T5_MAX_DIST / max_exact) * (nb - max_exact)).astype(jnp.int32)
    large = jnp.minimum(large, nb - 1)
    return jnp.where(rel > 0, nb, 0) + jnp.where(n < max_exact, n, large)


def _na_bases(rel_bias):
    rb = rel_bias.astype(F32)
    gap = jnp.full(rb.shape[:1] + (2 * NA_KH - 2, GRID_W - 2 * NA_KW + 1), -jnp.inf, F32)
    bases = jnp.concatenate(
        [rb[:, :-1, NA_KW - 1:], gap, rb[:, 1:, :], gap, rb[:, :-1, :NA_KW - 1]], axis=-1)
    return _by_pair(bases)


def _dil_bases(t5_bias):
    span = 2 * DIL_RADIUS + 1
    off = np.arange(span) - DIL_RADIUS
    dils = np.asarray([d for _, d in DIL_PATTERNS])
    bucket = _t5_bucket(jnp.asarray(off[None, :] * dils[:, None], jnp.int32))
    vec = jnp.take(t5_bias.astype(F32), bucket, axis=1)
    width = 2 * DIL_KBLOCK
    per_var = []
    for v in range(3):
        lo = DIL_RADIUS * (1 - v)
        pos = jnp.pad(vec[..., max(lo, 0):], ((0, 0), (0, 0), (max(-lo, 0), 0)),
                      constant_values=-jnp.inf)
        pos = jnp.pad(pos, ((0, 0), (0, 0), (0, width - pos.shape[-1] - max(lo, 0))),
                      constant_values=-jnp.inf)
        per_var.append(jnp.concatenate([pos, vec[..., :max(lo, 0)]], axis=-1))
    return _by_pair(jnp.stack(per_var, axis=2))


def _group_columns(w_qkv):
    w = w_qkv.reshape(D_MODEL, 3, N_HEADS_NA + N_HEADS_DIL, HEAD_DIM)
    w = w * jnp.asarray([HEAD_DIM ** -0.5, 1.0, 1.0], F32)[None, :, None, None]
    na = w[:, :, :N_HEADS_NA].reshape(D_MODEL, D_GROUP)
    dil = w[:, :, N_HEADS_NA:].reshape(D_MODEL, D_GROUP)
    return jnp.concatenate([na, dil], axis=1).astype(BF16)


def kernel(x, ffn1_pre_g, ffn1_w_gate, ffn1_w_up, ffn1_w_down, ffn1_post_g, mix_pre_g, w_qkv,
           na_rel_bias, t5_rel_bias, na_out_g, dil_out_g, w_out, mix_post_g, ffn2_pre_g,
           ffn2_w_gate, ffn2_w_up, ffn2_w_down, ffn2_post_g):
    b, s, d = x.shape
    depth = ffn1_pre_g.shape[0]
    dil_bases = _dil_bases(t5_rel_bias)
    ffn_params = lambda pre_g, wg, wu, wd, post_g: (
        pre_g[None], wg.astype(BF16), wu.astype(BF16), wd.astype(BF16), post_g[None])
    for l in range(depth):
        x = _ffn(x, ffn_params(ffn1_pre_g[l], ffn1_w_gate[l], ffn1_w_up[l], ffn1_w_down[l],
                               ffn1_post_g[l]))
        qkv_na, d1, d4, d16 = _qkv(x, mix_pre_g[l][None], _group_columns(w_qkv[l]))
        o_na = _na(qkv_na, _na_bases(na_rel_bias[l]))
        o_dil = _dil(d1, d4, d16, dil_bases)
        x = _mix_ffn(x, o_na, o_dil, na_out_g[l][None], dil_out_g[l][None],
                     w_out[l].astype(BF16), mix_post_g[l][None],
                     ffn_params(ffn2_pre_g[l], ffn2_w_gate[l], ffn2_w_up[l], ffn2_w_down[l],
                                ffn2_post_g[l]))
    return x
```

```python
import functools
import math

import jax
import jax.numpy as jnp
import numpy as np
from jax import lax
from jax.experimental import pallas as pl
from jax.experimental.pallas import tpu as pltpu

D_MODEL = 1024
HEAD_DIM = 64
N_HEADS_NA = 8
N_HEADS_DIL = 8
D_NA = N_HEADS_NA * HEAD_DIM
D_DIL = N_HEADS_DIL * HEAD_DIM
D_FF = 2816
GRID_W = 64
NA_KH = 8
NA_KW = 16
DIL_PATTERNS = ((128, 1), (512, 4), (2048, 16))
DIL_QBLOCK = 128
DIL_RADIUS = 64
DIL_KBLOCK = DIL_QBLOCK + 2 * DIL_RADIUS
DIL_VARIANTS = 3
T5_BUCKETS = 32
T5_MAX_DIST = 1024
NORM_EPS = 1e-6

LANES = 128
MXU_COLS = 256
FFN_ROWS = 128
HEADS_PER_TILE = LANES // HEAD_DIM
N_PAIRS = N_HEADS_NA // HEADS_PER_TILE
D_GROUP = 3 * D_NA
VMEM_LIMIT = 56 * 1024 * 1024

BF16 = jnp.bfloat16
F32 = jnp.float32


def _rms(x, g):
    return x * lax.rsqrt(jnp.mean(x * x, axis=-1, keepdims=True) + NORM_EPS) * g


def _resident(shape):
    return pl.BlockSpec(shape, lambda *_: (0,) * len(shape), pipeline_mode=pl.Buffered(1))


def _half_step_ffn(xs, pre_g_ref, wg_ref, wu_ref, wd_ref, post_g_ref):
    hs = [_rms(x, pre_g_ref[...]).astype(BF16) for x in xs]
    gus = [(jnp.dot(h, wg_ref[...], preferred_element_type=F32),
            jnp.dot(h, wu_ref[...], preferred_element_type=F32)) for h in hs]
    acts = [(g * jax.nn.sigmoid(g) * u).astype(BF16) for g, u in gus]
    ys = [jnp.dot(a, wd_ref[...], preferred_element_type=F32) for a in acts]
    return [x + 0.5 * _rms(y, post_g_ref[...]) for x, y in zip(xs, ys)]


def _row_tiles(n_rows):
    return [slice(r, r + FFN_ROWS) for r in range(0, n_rows, FFN_ROWS)]


def _ffn_kernel(x_ref, *refs):
    o_ref = refs[-1]
    tiles = _row_tiles(x_ref.shape[0])
    for t, out in zip(tiles, _half_step_ffn([x_ref[t, :] for t in tiles], *refs[:-1])):
        o_ref[t, :] = out


def _mix_ffn_kernel(x_ref, na_ref, dil_ref, g_na_ref, g_dil_ref, w_ref, g_mix_ref, *refs):
    o_ref = refs[-1]
    tiles = _row_tiles(x_ref.shape[0])
    heads = lambda ref, t: jnp.concatenate([ref[p, t, :] for p in range(N_PAIRS)], axis=-1)
    xs = []
    for t in tiles:
        a = _rms(heads(na_ref, t), g_na_ref[...]).astype(BF16)
        d = _rms(heads(dil_ref, t), g_dil_ref[...]).astype(BF16)
        mixed = (jnp.dot(a, w_ref[:D_NA, :], preferred_element_type=F32)
                 + jnp.dot(d, w_ref[D_NA:, :], preferred_element_type=F32))
        xs.append(x_ref[t, :] + _rms(mixed, g_mix_ref[...]))
    for t, out in zip(tiles, _half_step_ffn(xs, *refs[:-1])):
        o_ref[t, :] = out


def _ffn_specs():
    return [_resident((1, D_MODEL)), _resident((D_MODEL, D_FF)), _resident((D_MODEL, D_FF)),
            _resident((D_FF, D_MODEL)), _resident((1, D_MODEL))]


def _ffn(x, ffn_params, *, tm=512):
    b, s, _ = x.shape
    row = pl.BlockSpec((None, tm, D_MODEL), lambda bi, i: (bi, i, 0))
    return pl.pallas_call(
        _ffn_kernel,
        grid=(b, s // tm),
        in_specs=[row] + _ffn_specs(),
        out_specs=row,
        out_shape=jax.ShapeDtypeStruct((b, s, D_MODEL), F32),
        compiler_params=pltpu.CompilerParams(
            dimension_semantics=("arbitrary", "arbitrary"), vmem_limit_bytes=VMEM_LIMIT),
        name="ffn",
    )(x, *ffn_params)


def _mix_ffn(x, o_na, o_dil, g_na, g_dil, w, g_mix, ffn_params, *, tm=512):
    b, s, _ = x.shape
    row = pl.BlockSpec((None, tm, D_MODEL), lambda bi, i: (bi, i, 0))
    heads = pl.BlockSpec((None, N_PAIRS, tm, LANES), lambda bi, i: (bi, 0, i, 0))
    return pl.pallas_call(
        _mix_ffn_kernel,
        grid=(b, s // tm),
        in_specs=[row, heads, heads, _resident((1, D_NA)), _resident((1, D_DIL)),
                  _resident((D_NA + D_DIL, D_MODEL)), _resident((1, D_MODEL))] + _ffn_specs(),
        out_specs=row,
        out_shape=jax.ShapeDtypeStruct((b, s, D_MODEL), F32),
        compiler_params=pltpu.CompilerParams(
            dimension_semantics=("arbitrary", "arbitrary"), vmem_limit_bytes=VMEM_LIMIT),
        name="mix_ffn",
    )(x, o_na, o_dil, g_na, g_dil, w, g_mix, *ffn_params)


def _qkv_kernel(x_ref, g_ref, w_ref, na_ref, d1_ref, d4_ref, d16_ref, scr_ref, scr4_ref, *, tm):
    h = _rms(x_ref[0], g_ref[...]).astype(BF16)
    for c in range(0, D_GROUP, MXU_COLS):
        y = jnp.dot(h, w_ref[:, D_GROUP + c:D_GROUP + c + MXU_COLS], preferred_element_type=F32)
        for lo in range(0, MXU_COLS, LANES):
            t = (c + lo) // LANES
            y_t = y[:, lo:lo + LANES]
            d1_ref[0, t] = y_t.astype(BF16)
            scr_ref[t] = y_t
            for r in range(4):
                cls = scr_ref[t, pl.ds(r, tm // 4, stride=4), :]
                d4_ref[0, t, r] = cls.astype(BF16)
                scr4_ref[t, r] = cls
                for r2 in range(4):
                    d16_ref[0, t, 4 * r2 + r] = (
                        scr4_ref[t, r, pl.ds(r2, tm // 16, stride=4), :].astype(BF16))
    for c in range(0, D_GROUP, MXU_COLS):
        y = jnp.dot(h, w_ref[:, c:c + MXU_COLS], preferred_element_type=F32)
        for lo in range(0, MXU_COLS, LANES):
            na_ref[0, (c + lo) // LANES] = y[:, lo:lo + LANES].astype(BF16)


def _qkv(x, g, w, *, tm=512):
    b, s, _ = x.shape
    n_t = D_GROUP // LANES
    return pl.pallas_call(
        functools.partial(_qkv_kernel, tm=tm),
        grid=(b, s // tm),
        in_specs=[
            pl.BlockSpec((1, tm, D_MODEL), lambda bi, i: (bi, i, 0)),
            _resident((1, D_MODEL)),
            _resident((D_MODEL, 2 * D_GROUP)),
        ],
        out_specs=[
            pl.BlockSpec((1, n_t, tm, LANES), lambda bi, i: (bi, 0, i, 0)),
            pl.BlockSpec((1, n_t, tm, LANES), lambda bi, i: (bi, 0, i, 0)),
            pl.BlockSpec((1, n_t, 4, tm // 4, LANES), lambda bi, i: (bi, 0, 0, i, 0)),
            pl.BlockSpec((1, n_t, 16, tm // 16, LANES), lambda bi, i: (bi, 0, 0, i, 0)),
        ],
        out_shape=[
            jax.ShapeDtypeStruct((b, n_t, s, LANES), BF16),
            jax.ShapeDtypeStruct((b, n_t, s, LANES), BF16),
            jax.ShapeDtypeStruct((b, n_t, 4, s // 4, LANES), BF16),
            jax.ShapeDtypeStruct((b, n_t, 16, s // 16, LANES), BF16),
        ],
        scratch_shapes=[pltpu.VMEM((n_t, tm, LANES), F32),
                        pltpu.VMEM((n_t, 4, tm // 4, LANES), F32)],
        compiler_params=pltpu.CompilerParams(
            dimension_semantics=("arbitrary", "arbitrary"), vmem_limit_bytes=VMEM_LIMIT),
        name="qkv",
    )(x, g, w)


def _attend_blocks(blocks):
    lane = lax.broadcasted_iota(jnp.int32, (1, LANES), 1)
    second = lane >= HEAD_DIM
    scores = []
    for q, k, _, table in blocks:
        zero = jnp.zeros_like(q)
        q2 = jnp.concatenate([jnp.where(second, zero, q), jnp.where(second, q, zero)], axis=0)
        scores.append(lax.dot_general(q2, k, (((1,), (1,)), ((), ())),
                                      preferred_element_type=F32) + table)
    probs = []
    for s in scores:
        m = jnp.max(s, axis=-1, keepdims=True)
        p = jnp.exp(s - m)
        probs.append((m, jnp.sum(p, axis=-1, keepdims=True), p.astype(BF16)))
    out = []
    for (q, _, v, _), (m, l, p) in zip(blocks, probs):
        m_rows = q.shape[0]
        pv = jnp.dot(p, v, preferred_element_type=F32)
        pick = lambda a, m_rows=m_rows: jnp.where(second, a[m_rows:], a[:m_rows])
        out.append((pick(m), pick(l), pick(pv)))
    return out


def _toeplitz_rows(base_row, n_rows, first=0, step=1):
    base = jnp.broadcast_to(base_row, (n_rows, base_row.shape[-1]))
    return pltpu.roll(base, first, 1, stride=step, stride_axis=0)


def _na_kernel(q_ref, k_ref, v_ref, base_ref, o_ref, tbl_ref, *, unroll):
    rows = q_ref.shape[0] // GRID_W
    n_keys = NA_KH * GRID_W

    @pl.when(pl.program_id(1) == 0)
    def _():
        qc = lax.broadcasted_iota(jnp.int32, (GRID_W, LANES), 0)
        kc = lax.broadcasted_iota(jnp.int32, (GRID_W, LANES), 1) % GRID_W
        qs = jnp.clip(qc - NA_KW // 2, 0, GRID_W - NA_KW)
        valid = (kc >= qs) & (kc < qs + NA_KW)
        for h in range(HEADS_PER_TILE):
            for dr in range(2 * NA_KH - 2):
                tile = jnp.where(valid, _toeplitz_rows(base_ref[dr, h:h + 1, :], GRID_W), -jnp.inf)
                for t in range(n_keys // LANES):
                    v = NA_KH - 1 - dr + 2 * t
                    if 0 <= v < NA_KH:
                        tbl_ref[v, h * GRID_W:(h + 1) * GRID_W, t * LANES:(t + 1) * LANES] = tile

    def step(i, carry):
        blocks, starts = [], []
        for u in range(unroll):
            r = i * unroll + u
            kr0 = jnp.clip(r - NA_KH // 2, 0, rows - NA_KH)
            q0 = pl.multiple_of(r * GRID_W, GRID_W)
            k0 = pl.multiple_of(kr0 * GRID_W, GRID_W)
            starts.append(q0)
            blocks.append((q_ref[pl.ds(q0, GRID_W), :], k_ref[pl.ds(k0, n_keys), :],
                           v_ref[pl.ds(k0, n_keys), :], tbl_ref[r - kr0]))
        for q0, (_, l_t, acc_t) in zip(starts, _attend_blocks(blocks)):
            o_ref[pl.ds(q0, GRID_W), :] = acc_t / l_t
        return carry

    lax.fori_loop(0, rows // unroll, step, 0)


def _na(qkv_na, bases, *, unroll=8):
    b, _, s, _ = qkv_na.shape
    blk = lambda off: pl.BlockSpec((None, None, s, LANES), lambda p, bi: (bi, off + p, 0, 0))
    return pl.pallas_call(
        functools.partial(_na_kernel, unroll=unroll),
        grid=(N_PAIRS, b),
        in_specs=[
            blk(0), blk(N_PAIRS), blk(2 * N_PAIRS),
            pl.BlockSpec((None,) + bases.shape[1:], lambda p, bi: (p, 0, 0, 0)),
        ],
        out_specs=pl.BlockSpec((None, None, s, LANES), lambda p, bi: (bi, p, 0, 0)),
        out_shape=jax.ShapeDtypeStruct((b, N_PAIRS, s, LANES), F32),
        scratch_shapes=[pltpu.VMEM((NA_KH, HEADS_PER_TILE * GRID_W, NA_KH * GRID_W), F32)],
        compiler_params=pltpu.CompilerParams(
            dimension_semantics=("arbitrary", "arbitrary"), vmem_limit_bytes=VMEM_LIMIT),
        name="na_attn",
    )(qkv_na, qkv_na, qkv_na, bases)


def _dil_kernel(k1_ref, v1_ref, q4_ref, k4_ref, v4_ref, q16_ref, k16_ref, v16_ref,
                base_ref, o_ref, m_ref, l_ref, acc_ref, tbl_ref, *, unroll):
    s_len = k1_ref.shape[0]
    qb = DIL_QBLOCK
    kb = DIL_KBLOCK
    mid = DIL_PATTERNS[1][1]
    sub = qb // mid
    state = (m_ref, l_ref, acc_ref)

    @pl.when(pl.program_id(1) == 0)
    def _():
        for pat in range(len(DIL_PATTERNS)):
            for var in range(DIL_VARIANTS):
                for h in range(HEADS_PER_TILE):
                    base = base_ref[pat, var, h:h + 1, :]
                    if pat == 0:
                        for a in range(mid):
                            lo = h * qb + a * sub
                            tbl_ref[pat, var, lo:lo + sub, :] = (
                                _toeplitz_rows(base, sub, first=a, step=mid)[:, :kb])
                    else:
                        tbl_ref[pat, var, h * qb:(h + 1) * qb, :] = _toeplitz_rows(base, qb)[:, :kb]

    def window(i, n_blocks):
        q0 = pl.multiple_of(i * qb, qb)
        w0 = pl.multiple_of(jnp.clip(q0 - DIL_RADIUS, 0, n_blocks * qb - kb), DIL_RADIUS)
        var = jnp.where(i == 0, 0, jnp.where(i == n_blocks - 1, DIL_VARIANTS - 1, 1))
        return q0, w0, var

    def merge(old, new):
        m_new = jnp.maximum(old[0], new[0])
        a = jnp.exp(old[0] - m_new)
        bb = jnp.exp(new[0] - m_new)
        return m_new, a * old[1] + bb * new[1], a * old[2] + bb * new[2]

    def de_interleaved(pat, dil, q_ref, k_ref, v_ref):
        nb = s_len // dil // qb
        first = dil > mid

        def body(j, carry):
            blocks, dests = [], []
            for u in range(unroll):
                idx = j * unroll + u
                c = idx // nb
                q0, w0, var = window(idx % nb, nb)
                blocks.append((q_ref[c, pl.ds(q0, qb), :], k_ref[c, pl.ds(w0, kb), :],
                               v_ref[c, pl.ds(w0, kb), :], tbl_ref[pat, var]))
                step = dil // mid
                dests.append((c % mid, pl.ds(c // mid + q0 * step, qb, stride=step) if first
                              else pl.ds(q0, qb)))
            for (r, rows), pieces in zip(dests, _attend_blocks(blocks)):
                if not first:
                    pieces = merge([ref[r, rows, :] for ref in state], pieces)
                for ref, piece in zip(state, pieces):
                    ref[r, rows, :] = piece
            return carry

        lax.fori_loop(0, dil * nb // unroll, body, 0)

    de_interleaved(2, DIL_PATTERNS[2][1], q16_ref, k16_ref, v16_ref)
    de_interleaved(1, mid, q4_ref, k4_ref, v4_ref)

    nb1 = s_len // qb
    gather = lambda ref, p0: jnp.concatenate(
        [ref[a, pl.ds(p0, sub), :] for a in range(mid)], axis=0)

    def p1(j, carry):
        blocks, starts = [], []
        for u in range(unroll):
            i = j * unroll + u
            q0, w0, var = window(i, nb1)
            p0 = pl.multiple_of(i * sub, sub)
            starts.append((q0, p0))
            blocks.append((gather(q4_ref, p0), k1_ref[pl.ds(w0, kb), :],
                           v1_ref[pl.ds(w0, kb), :], tbl_ref[0, var]))
        for (q0, p0), pieces in zip(starts, _attend_blocks(blocks)):
            _, l_n, acc_n = merge([gather(ref, p0) for ref in state], pieces)
            out = acc_n / l_n
            for a in range(mid):
                o_ref[pl.ds(q0 + a, sub, stride=mid), :] = out[a * sub:(a + 1) * sub]
        return carry

    lax.fori_loop(0, nb1 // unroll, p1, 0)


def _dil(d1, d4, d16, bases, *, unroll=4):
    b, _, s, _ = d1.shape
    mid = DIL_PATTERNS[1][1]
    blk1 = lambda off: pl.BlockSpec((None, None, s, LANES), lambda p, bi: (bi, off + p, 0, 0))
    blkd = lambda d, off: pl.BlockSpec((None, None, d, s // d, LANES),
                                       lambda p, bi: (bi, off + p, 0, 0, 0))
    specs = [blk1(N_PAIRS), blk1(2 * N_PAIRS)]
    for d in (mid, DIL_PATTERNS[2][1]):
        specs += [blkd(d, 0), blkd(d, N_PAIRS), blkd(d, 2 * N_PAIRS)]
    specs.append(pl.BlockSpec((None,) + bases.shape[1:], lambda p, bi: (p, 0, 0, 0, 0)))
    n_pat, n_var = bases.shape[1:3]
    return pl.pallas_call(
        functools.partial(_dil_kernel, unroll=unroll),
        grid=(N_PAIRS, b),
        in_specs=specs,
        out_specs=pl.BlockSpec((None, None, s, LANES), lambda p, bi: (bi, p, 0, 0)),
        out_shape=jax.ShapeDtypeStruct((b, N_PAIRS, s, LANES), F32),
        scratch_shapes=[pltpu.VMEM((mid, s // mid, LANES), F32)] * 3
        + [pltpu.VMEM((n_pat, n_var, HEADS_PER_TILE * DIL_QBLOCK, DIL_KBLOCK), F32)],
        compiler_params=pltpu.CompilerParams(
            dimension_semantics=("arbitrary", "arbitrary"), vmem_limit_bytes=VMEM_LIMIT),
        name="dil_attn",
    )(d1, d1, d4, d4, d4, d16, d16, d16, bases)


def _by_pair(bases):
    t = bases.reshape((N_PAIRS, HEADS_PER_TILE) + bases.shape[1:])
    return jnp.moveaxis(t, 1, -2)


def _t5_bucket(rel):
    nb = T5_BUCKETS // 2
    max_exact = nb // 2
    n = jnp.abs(rel)
    large = max_exact + (jnp.log(jnp.maximum(n, 1).astype(F32) / max_exact)
                         / math.log(T5_MAX_DIST / max_exact) * (nb - max_exact)).astype(jnp.int32)
    large = jnp.minimum(large, nb - 1)
    return jnp.where(rel > 0, nb, 0) + jnp.where(n < max_exact, n, large)


def _na_bases(rel_bias):
    rb = rel_bias.astype(F32)
    gap = jnp.full(rb.shape[:1] + (2 * NA_KH - 2, GRID_W - 2 * NA_KW + 1), -jnp.inf, F32)
    bases = jnp.concatenate(
        [rb[:, :-1, NA_KW - 1:], gap, rb[:, 1:, :], gap, rb[:, :-1, :NA_KW - 1]], axis=-1)
    return _by_pair(bases)


def _dil_bases(t5_bias):
    span = 2 * DIL_RADIUS + 1
    off = np.arange(span) - DIL_RADIUS
    dils = np.asarray([d for _, d in DIL_PATTERNS])
    bucket = _t5_bucket(jnp.asarray(off[None, :] * dils[:, None], jnp.int32))
    vec = jnp.take(t5_bias.astype(F32), bucket, axis=1)
    width = 2 * DIL_KBLOCK
    per_var = []
    for v in range(DIL_VARIANTS):
        lo = DIL_RADIUS * (1 - v)
        pos = jnp.pad(vec[..., max(lo, 0):], ((0, 0), (0, 0), (max(-lo, 0), 0)),
                      constant_values=-jnp.inf)
        pos = jnp.pad(pos, ((0, 0), (0, 0), (0, width - pos.shape[-1] - max(lo, 0))),
                      constant_values=-jnp.inf)
        per_var.append(jnp.concatenate([pos, vec[..., :max(lo, 0)]], axis=-1))
    return _by_pair(jnp.stack(per_var, axis=2))


def _group_columns(w_qkv):
    w = w_qkv.reshape(D_MODEL, 3, N_HEADS_NA + N_HEADS_DIL, HEAD_DIM)
    w = w * jnp.asarray([HEAD_DIM ** -0.5, 1.0, 1.0], F32)[None, :, None, None]
    na = w[:, :, :N_HEADS_NA].reshape(D_MODEL, D_GROUP)
    dil = w[:, :, N_HEADS_NA:].reshape(D_MODEL, D_GROUP)
    return jnp.concatenate([na, dil], axis=1).astype(BF16)


def kernel(x, ffn1_pre_g, ffn1_w_gate, ffn1_w_up, ffn1_w_down, ffn1_post_g, mix_pre_g, w_qkv,
           na_rel_bias, t5_rel_bias, na_out_g, dil_out_g, w_out, mix_post_g, ffn2_pre_g,
           ffn2_w_gate, ffn2_w_up, ffn2_w_down, ffn2_post_g):
    b, s, d = x.shape
    depth = ffn1_pre_g.shape[0]
    dil_bases = _dil_bases(t5_rel_bias)
    ffn_params = lambda pre_g, wg, wu, wd, post_g: (
        pre_g[None], wg.astype(BF16), wu.astype(BF16), wd.astype(BF16), post_g[None])
    for l in range(depth):
        x = _ffn(x, ffn_params(ffn1_pre_g[l], ffn1_w_gate[l], ffn1_w_up[l], ffn1_w_down[l],
                               ffn1_post_g[l]))
        qkv_na, d1, d4, d16 = _qkv(x, mix_pre_g[l][None], _group_columns(w_qkv[l]))
        o_na = _na(qkv_na, _na_bases(na_rel_bias[l]))
        o_dil = _dil(d1, d4, d16, dil_bases)
        x = _mix_ffn(x, o_na, o_dil, na_out_g[l][None], dil_out_g[l][None],
                     w_out[l].astype(BF16), mix_post_g[l][None],
                     ffn_params(ffn2_pre_g[l], ffn2_w_gate[l], ffn2_w_up[l], ffn2_w_down[l],
                                ffn2_post_g[l]))
    return x
```

```python
import functools
import math

import jax
import jax.numpy as jnp
import numpy as np
from jax import lax
from jax.experimental import pallas as pl
from jax.experimental.pallas import tpu as pltpu

D_MODEL = 1024
HEAD_DIM = 64
N_HEADS_NA = 8
N_HEADS_DIL = 8
D_NA = N_HEADS_NA * HEAD_DIM
D_DIL = N_HEADS_DIL * HEAD_DIM
D_FF = 2816
GRID_W = 64
NA_KH = 8
NA_KW = 16
DIL_PATTERNS = ((128, 1), (512, 4), (2048, 16))
DIL_QBLOCK = 128
DIL_RADIUS = 64
DIL_KBLOCK = DIL_QBLOCK + 2 * DIL_RADIUS
DIL_VARIANTS = 3
T5_BUCKETS = 32
T5_MAX_DIST = 1024
NORM_EPS = 1e-6

LANES = 128
MXU_COLS = 256
FFN_ROWS = 128
HEADS_PER_TILE = LANES // HEAD_DIM
N_PAIRS = N_HEADS_NA // HEADS_PER_TILE
D_GROUP = 3 * D_NA
VMEM_LIMIT = 56 * 1024 * 1024

BF16 = jnp.bfloat16
F32 = jnp.float32


def _rms(x, g):
    return x * lax.rsqrt(jnp.mean(x * x, axis=-1, keepdims=True) + NORM_EPS) * g


def _resident(shape):
    return pl.BlockSpec(shape, lambda *_: (0,) * len(shape), pipeline_mode=pl.Buffered(1))


def _half_step_ffn(xs, pre_g_ref, wg_ref, wu_ref, wd_ref, post_g_ref):
    hs = [_rms(x, pre_g_ref[...]).astype(BF16) for x in xs]
    gus = [(jnp.dot(h, wg_ref[...], preferred_element_type=F32),
            jnp.dot(h, wu_ref[...], preferred_element_type=F32)) for h in hs]
    acts = [(g * jax.nn.sigmoid(g) * u).astype(BF16) for g, u in gus]
    ys = [jnp.dot(a, wd_ref[...], preferred_element_type=F32) for a in acts]
    return [x + 0.5 * _rms(y, post_g_ref[...]) for x, y in zip(xs, ys)]


def _row_tiles(n_rows):
    return [slice(r, r + FFN_ROWS) for r in range(0, n_rows, FFN_ROWS)]


def _ffn_kernel(x_ref, *refs):
    o_ref = refs[-1]
    tiles = _row_tiles(x_ref.shape[0])
    for t, out in zip(tiles, _half_step_ffn([x_ref[t, :] for t in tiles], *refs[:-1])):
        o_ref[t, :] = out


def _mix_ffn_kernel(x_ref, na_ref, dil_ref, g_na_ref, g_dil_ref, w_ref, g_mix_ref, *refs):
    o_ref = refs[-1]
    tiles = _row_tiles(x_ref.shape[0])
    heads = lambda ref, t: jnp.concatenate([ref[p, t, :] for p in range(N_PAIRS)], axis=-1)
    xs = []
    for t in tiles:
        a = _rms(heads(na_ref, t), g_na_ref[...]).astype(BF16)
        d = _rms(heads(dil_ref, t), g_dil_ref[...]).astype(BF16)
        mixed = (jnp.dot(a, w_ref[:D_NA, :], preferred_element_type=F32)
                 + jnp.dot(d, w_ref[D_NA:, :], preferred_element_type=F32))
        xs.append(x_ref[t, :] + _rms(mixed, g_mix_ref[...]))
    for t, out in zip(tiles, _half_step_ffn(xs, *refs[:-1])):
        o_ref[t, :] = out


def _ffn_specs():
    return [_resident((1, D_MODEL)), _resident((D_MODEL, D_FF)), _resident((D_MODEL, D_FF)),
            _resident((D_FF, D_MODEL)), _resident((1, D_MODEL))]


def _ffn(x, ffn_params, *, tm=512):
    b, s, _ = x.shape
    row = pl.BlockSpec((None, tm, D_MODEL), lambda bi, i: (bi, i, 0))
    return pl.pallas_call(
        _ffn_kernel,
        grid=(b, s // tm),
        in_specs=[row] + _ffn_specs(),
        out_specs=row,
        out_shape=jax.ShapeDtypeStruct((b, s, D_MODEL), F32),
        compiler_params=pltpu.CompilerParams(
            dimension_semantics=("arbitrary", "arbitrary"), vmem_limit_bytes=VMEM_LIMIT),
        name="ffn",
    )(x, *ffn_params)


def _mix_ffn(x, o_na, o_dil, g_na, g_dil, w, g_mix, ffn_params, *, tm=512):
    b, s, _ = x.shape
    row = pl.BlockSpec((None, tm, D_MODEL), lambda bi, i: (bi, i, 0))
    heads = pl.BlockSpec((None, N_PAIRS, tm, LANES), lambda bi, i: (bi, 0, i, 0))
    return pl.pallas_call(
        _mix_ffn_kernel,
        grid=(b, s // tm),
        in_specs=[row, heads, heads, _resident((1, D_NA)), _resident((1, D_DIL)),
                  _resident((D_NA + D_DIL, D_MODEL)), _resident((1, D_MODEL))] + _ffn_specs(),
        out_specs=row,
        out_shape=jax.ShapeDtypeStruct((b, s, D_MODEL), F32),
        compiler_params=pltpu.CompilerParams(
            dimension_semantics=("arbitrary", "arbitrary"), vmem_limit_bytes=VMEM_LIMIT),
        name="mix_ffn",
    )(x, o_na, o_dil, g_na, g_dil, w, g_mix, *ffn_params)


def _qkv_kernel(x_ref, g_ref, w_ref, na_ref, d1_ref, d4_ref, d16_ref, scr_ref, scr4_ref, *, tm):
    h = _rms(x_ref[0], g_ref[...]).astype(BF16)
    for c in range(0, D_GROUP, MXU_COLS):
        y = jnp.dot(h, w_ref[:, D_GROUP + c:D_GROUP + c + MXU_COLS], preferred_element_type=F32)
        for lo in range(0, MXU_COLS, LANES):
            t = (c + lo) // LANES
            y_t = y[:, lo:lo + LANES]
            if t >= N_PAIRS:
                d1_ref[0, t - N_PAIRS] = y_t.astype(BF16)
            scr_ref[t] = y_t
            for r in range(4):
                cls = scr_ref[t, pl.ds(r, tm // 4, stride=4), :]
                d4_ref[0, t, r] = cls.astype(BF16)
                scr4_ref[t, r] = cls
                for r2 in range(4):
                    d16_ref[0, t, 4 * r2 + r] = (
                        scr4_ref[t, r, pl.ds(r2, tm // 16, stride=4), :].astype(BF16))
    for c in range(0, D_GROUP, MXU_COLS):
        y = jnp.dot(h, w_ref[:, c:c + MXU_COLS], preferred_element_type=F32)
        for lo in range(0, MXU_COLS, LANES):
            na_ref[0, (c + lo) // LANES] = y[:, lo:lo + LANES].astype(BF16)


def _qkv(x, g, w, *, tm=512):
    b, s, _ = x.shape
    n_t = D_GROUP // LANES
    return pl.pallas_call(
        functools.partial(_qkv_kernel, tm=tm),
        grid=(b, s // tm),
        in_specs=[
            pl.BlockSpec((1, tm, D_MODEL), lambda bi, i: (bi, i, 0)),
            _resident((1, D_MODEL)),
            _resident((D_MODEL, 2 * D_GROUP)),
        ],
        out_specs=[
            pl.BlockSpec((1, n_t, tm, LANES), lambda bi, i: (bi, 0, i, 0)),
            pl.BlockSpec((1, n_t - N_PAIRS, tm, LANES), lambda bi, i: (bi, 0, i, 0)),
            pl.BlockSpec((1, n_t, 4, tm // 4, LANES), lambda bi, i: (bi, 0, 0, i, 0)),
            pl.BlockSpec((1, n_t, 16, tm // 16, LANES), lambda bi, i: (bi, 0, 0, i, 0)),
        ],
        out_shape=[
            jax.ShapeDtypeStruct((b, n_t, s, LANES), BF16),
            jax.ShapeDtypeStruct((b, n_t - N_PAIRS, s, LANES), BF16),
            jax.ShapeDtypeStruct((b, n_t, 4, s // 4, LANES), BF16),
            jax.ShapeDtypeStruct((b, n_t, 16, s // 16, LANES), BF16),
        ],
        scratch_shapes=[pltpu.VMEM((n_t, tm, LANES), F32),
                        pltpu.VMEM((n_t, 4, tm // 4, LANES), F32)],
        compiler_params=pltpu.CompilerParams(
            dimension_semantics=("arbitrary", "arbitrary"), vmem_limit_bytes=VMEM_LIMIT),
        name="qkv",
    )(x, g, w)


def _attend_blocks(blocks):
    lane = lax.broadcasted_iota(jnp.int32, (1, LANES), 1)
    second = lane >= HEAD_DIM
    scores = []
    for q, k, _, table in blocks:
        zero = jnp.zeros_like(q)
        q2 = jnp.concatenate([jnp.where(second, zero, q), jnp.where(second, q, zero)], axis=0)
        scores.append(lax.dot_general(q2, k, (((1,), (1,)), ((), ())),
                                      preferred_element_type=F32) + table)
    probs = []
    for s in scores:
        m = jnp.max(s, axis=-1, keepdims=True)
        p = jnp.exp(s - m)
        probs.append((m, jnp.sum(p, axis=-1, keepdims=True), p.astype(BF16)))
    out = []
    for (q, _, v, _), (m, l, p) in zip(blocks, probs):
        m_rows = q.shape[0]
        pv = jnp.dot(p, v, preferred_element_type=F32)
        pick = lambda a, m_rows=m_rows: jnp.where(second, a[m_rows:], a[:m_rows])
        out.append((pick(m), pick(l), pick(pv)))
    return out


def _toeplitz_rows(base_row, n_rows, first=0, step=1):
    base = jnp.broadcast_to(base_row, (n_rows, base_row.shape[-1]))
    return pltpu.roll(base, first, 1, stride=step, stride_axis=0)


def _na_kernel(q_ref, k_ref, v_ref, base_ref, o_ref, tbl_ref, *, unroll):
    rows = q_ref.shape[0] // GRID_W
    n_keys = NA_KH * GRID_W

    @pl.when(pl.program_id(1) == 0)
    def _():
        qc = lax.broadcasted_iota(jnp.int32, (GRID_W, LANES), 0)
        kc = lax.broadcasted_iota(jnp.int32, (GRID_W, LANES), 1) % GRID_W
        qs = jnp.clip(qc - NA_KW // 2, 0, GRID_W - NA_KW)
        valid = (kc >= qs) & (kc < qs + NA_KW)
        for h in range(HEADS_PER_TILE):
            for dr in range(2 * NA_KH - 2):
                tile = jnp.where(valid, _toeplitz_rows(base_ref[dr, h:h + 1, :], GRID_W), -jnp.inf)
                for t in range(n_keys // LANES):
                    v = NA_KH - 1 - dr + 2 * t
                    if 0 <= v < NA_KH:
                        tbl_ref[v, h * GRID_W:(h + 1) * GRID_W, t * LANES:(t + 1) * LANES] = tile

    def step(i, carry):
        blocks, starts = [], []
        for u in range(unroll):
            r = i * unroll + u
            kr0 = jnp.clip(r - NA_KH // 2, 0, rows - NA_KH)
            q0 = pl.multiple_of(r * GRID_W, GRID_W)
            k0 = pl.multiple_of(kr0 * GRID_W, GRID_W)
            starts.append(q0)
            blocks.append((q_ref[pl.ds(q0, GRID_W), :], k_ref[pl.ds(k0, n_keys), :],
                           v_ref[pl.ds(k0, n_keys), :], tbl_ref[r - kr0]))
        for q0, (_, l_t, acc_t) in zip(starts, _attend_blocks(blocks)):
            o_ref[pl.ds(q0, GRID_W), :] = acc_t / l_t
        return carry

    lax.fori_loop(0, rows // unroll, step, 0)


def _na(qkv_na, bases, *, unroll=8):
    b, _, s, _ = qkv_na.shape
    blk = lambda off: pl.BlockSpec((None, None, s, LANES), lambda p, bi: (bi, off + p, 0, 0))
    return pl.pallas_call(
        functools.partial(_na_kernel, unroll=unroll),
        grid=(N_PAIRS, b),
        in_specs=[
            blk(0), blk(N_PAIRS), blk(2 * N_PAIRS),
            pl.BlockSpec((None,) + bases.shape[1:], lambda p, bi: (p, 0, 0, 0)),
        ],
        out_specs=pl.BlockSpec((None, None, s, LANES), lambda p, bi: (bi, p, 0, 0)),
        out_shape=jax.ShapeDtypeStruct((b, N_PAIRS, s, LANES), F32),
        scratch_shapes=[pltpu.VMEM((NA_KH, HEADS_PER_TILE * GRID_W, NA_KH * GRID_W), F32)],
        compiler_params=pltpu.CompilerParams(
            dimension_semantics=("arbitrary", "arbitrary"), vmem_limit_bytes=VMEM_LIMIT),
        name="na_attn",
    )(qkv_na, qkv_na, qkv_na, bases)


def _dil_kernel(k1_ref, v1_ref, q4_ref, k4_ref, v4_ref, q16_ref, k16_ref, v16_ref,
                base_ref, o_ref, m_ref, l_ref, acc_ref, tbl_ref, *, unroll):
    s_len = k1_ref.shape[0]
    qb = DIL_QBLOCK
    kb = DIL_KBLOCK
    mid = DIL_PATTERNS[1][1]
    sub = qb // mid
    state = (m_ref, l_ref, acc_ref)

    @pl.when(pl.program_id(1) == 0)
    def _():
        for pat in range(len(DIL_PATTERNS)):
            for var in range(DIL_VARIANTS):
                for h in range(HEADS_PER_TILE):
                    base = base_ref[pat, var, h:h + 1, :]
                    if pat == 0:
                        for a in range(mid):
                            lo = h * qb + a * sub
                            tbl_ref[pat, var, lo:lo + sub, :] = (
                                _toeplitz_rows(base, sub, first=a, step=mid)[:, :kb])
                    else:
                        tbl_ref[pat, var, h * qb:(h + 1) * qb, :] = _toeplitz_rows(base, qb)[:, :kb]

    def window(i, n_blocks):
        q0 = pl.multiple_of(i * qb, qb)
        w0 = pl.multiple_of(jnp.clip(q0 - DIL_RADIUS, 0, n_blocks * qb - kb), DIL_RADIUS)
        var = jnp.where(i == 0, 0, jnp.where(i == n_blocks - 1, DIL_VARIANTS - 1, 1))
        return q0, w0, var

    def merge(old, new):
        m_new = jnp.maximum(old[0], new[0])
        a = jnp.exp(old[0] - m_new)
        bb = jnp.exp(new[0] - m_new)
        return m_new, a * old[1] + bb * new[1], a * old[2] + bb * new[2]

    def de_interleaved(pat, dil, q_ref, k_ref, v_ref):
        nb = s_len // dil // qb
        first = dil > mid

        def body(j, carry):
            blocks, dests = [], []
            for u in range(unroll):
                idx = j * unroll + u
                c = idx // nb
                q0, w0, var = window(idx % nb, nb)
                blocks.append((q_ref[c, pl.ds(q0, qb), :], k_ref[c, pl.ds(w0, kb), :],
                               v_ref[c, pl.ds(w0, kb), :], tbl_ref[pat, var]))
                step = dil // mid
                dests.append((c % mid, pl.ds(c // mid + q0 * step, qb, stride=step) if first
                              else pl.ds(q0, qb)))
            for (r, rows), pieces in zip(dests, _attend_blocks(blocks)):
                if not first:
                    pieces = merge([ref[r, rows, :] for ref in state], pieces)
                for ref, piece in zip(state, pieces):
                    ref[r, rows, :] = piece
            return carry

        lax.fori_loop(0, dil * nb // unroll, body, 0)

    de_interleaved(2, DIL_PATTERNS[2][1], q16_ref, k16_ref, v16_ref)
    de_interleaved(1, mid, q4_ref, k4_ref, v4_ref)

    nb1 = s_len // qb
    gather = lambda ref, p0: jnp.concatenate(
        [ref[a, pl.ds(p0, sub), :] for a in range(mid)], axis=0)

    def p1(j, carry):
        blocks, starts = [], []
        for u in range(unroll):
            i = j * unroll + u
            q0, w0, var = window(i, nb1)
            p0 = pl.multiple_of(i * sub, sub)
            starts.append((q0, p0))
            blocks.append((gather(q4_ref, p0), k1_ref[pl.ds(w0, kb), :],
                           v1_ref[pl.ds(w0, kb), :], tbl_ref[0, var]))
        for (q0, p0), pieces in zip(starts, _attend_blocks(blocks)):
            _, l_n, acc_n = merge([gather(ref, p0) for ref in state], pieces)
            out = acc_n / l_n
            for a in range(mid):
                o_ref[pl.ds(q0 + a, sub, stride=mid), :] = out[a * sub:(a + 1) * sub]
        return carry

    lax.fori_loop(0, nb1 // unroll, p1, 0)


def _dil(d1, d4, d16, bases, *, unroll=4):
    b, _, s, _ = d1.shape
    mid = DIL_PATTERNS[1][1]
    blk1 = lambda off: pl.BlockSpec((None, None, s, LANES), lambda p, bi: (bi, off + p, 0, 0))
    blkd = lambda d, off: pl.BlockSpec((None, None, d, s // d, LANES),
                                       lambda p, bi: (bi, off + p, 0, 0, 0))
    specs = [blk1(0), blk1(N_PAIRS)]
    for d in (mid, DIL_PATTERNS[2][1]):
        specs += [blkd(d, 0), blkd(d, N_PAIRS), blkd(d, 2 * N_PAIRS)]
    specs.append(pl.BlockSpec((None,) + bases.shape[1:], lambda p, bi: (p, 0, 0, 0, 0)))
    n_pat, n_var = bases.shape[1:3]
    return pl.pallas_call(
        functools.partial(_dil_kernel, unroll=unroll),
        grid=(N_PAIRS, b),
        in_specs=specs,
        out_specs=pl.BlockSpec((None, None, s, LANES), lambda p, bi: (bi, p, 0, 0)),
        out_shape=jax.ShapeDtypeStruct((b, N_PAIRS, s, LANES), F32),
        scratch_shapes=[pltpu.VMEM((mid, s // mid, LANES), F32)] * 3
        + [pltpu.VMEM((n_pat, n_var, HEADS_PER_TILE * DIL_QBLOCK, DIL_KBLOCK), F32)],
        compiler_params=pltpu.CompilerParams(
            dimension_semantics=("arbitrary", "arbitrary"), vmem_limit_bytes=VMEM_LIMIT),
        name="dil_attn",
    )(d1, d1, d4, d4, d4, d16, d16, d16, bases)


def _by_pair(bases):
    t = bases.reshape((N_PAIRS, HEADS_PER_TILE) + bases.shape[1:])
    return jnp.moveaxis(t, 1, -2)


def _t5_bucket(rel):
    nb = T5_BUCKETS // 2
    max_exact = nb // 2
    n = jnp.abs(rel)
    large = max_exact + (jnp.log(jnp.maximum(n, 1).astype(F32) / max_exact)
                         / math.log(T5_MAX_DIST / max_exact) * (nb - max_exact)).astype(jnp.int32)
    large = jnp.minimum(large, nb - 1)
    return jnp.where(rel > 0, nb, 0) + jnp.where(n < max_exact, n, large)


def _na_bases(rel_bias):
    rb = rel_bias.astype(F32)
    gap = jnp.full(rb.shape[:1] + (2 * NA_KH - 2, GRID_W - 2 * NA_KW + 1), -jnp.inf, F32)
    bases = jnp.concatenate(
        [rb[:, :-1, NA_KW - 1:], gap, rb[:, 1:, :], gap, rb[:, :-1, :NA_KW - 1]], axis=-1)
    return _by_pair(bases)


def _dil_bases(t5_bias):
    span = 2 * DIL_RADIUS + 1
    off = np.arange(span) - DIL_RADIUS
    dils = np.asarray([d for _, d in DIL_PATTERNS])
    bucket = _t5_bucket(jnp.asarray(off[None, :] * dils[:, None], jnp.int32))
    vec = jnp.take(t5_bias.astype(F32), bucket, axis=1)
    width = 2 * DIL_KBLOCK
    per_var = []
    for v in range(DIL_VARIANTS):
        lo = DIL_RADIUS * (1 - v)
        pos = jnp.pad(vec[..., max(lo, 0):], ((0, 0), (0, 0), (max(-lo, 0), 0)),
                      constant_values=-jnp.inf)
        pos = jnp.pad(pos, ((0, 0), (0, 0), (0, width - pos.shape[-1] - max(lo, 0))),
                      constant_values=-jnp.inf)
        per_var.append(jnp.concatenate([pos, vec[..., :max(lo, 0)]], axis=-1))
    return _by_pair(jnp.stack(per_var, axis=2))


def _cast_kernel(*refs):
    n = len(refs) // 2
    for src, dst in zip(refs[:n], refs[n:]):
        dst[...] = src[...].astype(BF16)


def _cast_bf16(arrays, *, n_steps=8):
    specs = [pl.BlockSpec((a.shape[0] // n_steps, a.shape[1]), lambda i: (i, 0)) for a in arrays]
    return pl.pallas_call(
        _cast_kernel,
        grid=(n_steps,),
        in_specs=specs,
        out_specs=specs,
        out_shape=[jax.ShapeDtypeStruct(a.shape, BF16) for a in arrays],
        compiler_params=pltpu.CompilerParams(
            dimension_semantics=("arbitrary",), vmem_limit_bytes=VMEM_LIMIT),
        name="cast_weights",
    )(*arrays)


def _group_columns_kernel(w_ref, o_ref):
    is_q = pl.program_id(0) % 3 == 0
    o_ref[...] = (w_ref[...] * jnp.where(is_q, HEAD_DIM ** -0.5, 1.0)).astype(BF16)


def _group_columns(w_qkv):
    return pl.pallas_call(
        _group_columns_kernel,
        grid=(2 * 3,),
        in_specs=[pl.BlockSpec((D_MODEL, D_NA), lambda j: (0, (j % 3) * 2 + j // 3))],
        out_specs=pl.BlockSpec((D_MODEL, D_NA), lambda j: (0, j)),
        out_shape=jax.ShapeDtypeStruct((D_MODEL, 2 * D_GROUP), BF16),
        compiler_params=pltpu.CompilerParams(
            dimension_semantics=("arbitrary",), vmem_limit_bytes=VMEM_LIMIT),
        name="group_qkv_columns",
    )(w_qkv)


def kernel(x, ffn1_pre_g, ffn1_w_gate, ffn1_w_up, ffn1_w_down, ffn1_post_g, mix_pre_g, w_qkv,
           na_rel_bias, t5_rel_bias, na_out_g, dil_out_g, w_out, mix_post_g, ffn2_pre_g,
           ffn2_w_gate, ffn2_w_up, ffn2_w_down, ffn2_post_g):
    b, s, d = x.shape
    depth = ffn1_pre_g.shape[0]
    dil_bases = _dil_bases(t5_rel_bias)
    for l in range(depth):
        wg1, wu1, wd1, wg2, wu2, wd2, wo = _cast_bf16(
            [ffn1_w_gate[l], ffn1_w_up[l], ffn1_w_down[l],
             ffn2_w_gate[l], ffn2_w_up[l], ffn2_w_down[l], w_out[l]])
        x = _ffn(x, (ffn1_pre_g[l][None], wg1, wu1, wd1, ffn1_post_g[l][None]))
        qkv_na, d1, d4, d16 = _qkv(x, mix_pre_g[l][None], _group_columns(w_qkv[l]))
        o_na = _na(qkv_na, _na_bases(na_rel_bias[l]))
        o_dil = _dil(d1, d4, d16, dil_bases)
        x = _mix_ffn(x, o_na, o_dil, na_out_g[l][None], dil_out_g[l][None], wo,
                     mix_post_g[l][None],
                     (ffn2_pre_g[l][None], wg2, wu2, wd2, ffn2_post_g[l][None]))
    return x
```

```python
import functools
import math

import jax
import jax.numpy as jnp
import numpy as np
from jax import lax
from jax.experimental import pallas as pl
from jax.experimental.pallas import tpu as pltpu

D_MODEL = 1024
HEAD_DIM = 64
N_HEADS_NA = 8
N_HEADS_DIL = 8
D_NA = N_HEADS_NA * HEAD_DIM
D_DIL = N_HEADS_DIL * HEAD_DIM
D_FF = 2816
GRID_W = 64
NA_KH = 8
NA_KW = 16
DIL_PATTERNS = ((128, 1), (512, 4), (2048, 16))
DIL_QBLOCK = 128
DIL_RADIUS = 64
DIL_KBLOCK = DIL_QBLOCK + 2 * DIL_RADIUS
DIL_VARIANTS = 3
T5_BUCKETS = 32
T5_MAX_DIST = 1024
NORM_EPS = 1e-6

LANES = 128
MXU_COLS = 256
FFN_ROWS = 128
HEADS_PER_TILE = LANES // HEAD_DIM
N_PAIRS = N_HEADS_NA // HEADS_PER_TILE
D_GROUP = 3 * D_NA
VMEM_LIMIT = 56 * 1024 * 1024

BF16 = jnp.bfloat16
F32 = jnp.float32


def _rms(x, g):
    return x * lax.rsqrt(jnp.mean(x * x, axis=-1, keepdims=True) + NORM_EPS) * g


def _resident(shape):
    return pl.BlockSpec(shape, lambda *_: (0,) * len(shape), pipeline_mode=pl.Buffered(1))


def _half_step_ffn(xs, pre_g_ref, wg_ref, wu_ref, wd_ref, post_g_ref):
    hs = [_rms(x, pre_g_ref[...]).astype(BF16) for x in xs]
    gus = [(jnp.dot(h, wg_ref[...], preferred_element_type=F32),
            jnp.dot(h, wu_ref[...], preferred_element_type=F32)) for h in hs]
    acts = [(g * jax.nn.sigmoid(g) * u).astype(BF16) for g, u in gus]
    ys = [jnp.dot(a, wd_ref[...], preferred_element_type=F32) for a in acts]
    return [x + 0.5 * _rms(y, post_g_ref[...]) for x, y in zip(xs, ys)]


def _row_tiles(n_rows):
    return [slice(r, r + FFN_ROWS) for r in range(0, n_rows, FFN_ROWS)]


def _ffn_kernel(x_ref, *refs):
    o_ref = refs[-1]
    tiles = _row_tiles(x_ref.shape[0])
    for t, out in zip(tiles, _half_step_ffn([x_ref[t, :] for t in tiles], *refs[:-1])):
        o_ref[t, :] = out


def _mix_ffn_kernel(x_ref, na_ref, dil_ref, g_na_ref, g_dil_ref, w_ref, g_mix_ref, *refs):
    o_ref = refs[-1]
    tiles = _row_tiles(x_ref.shape[0])
    heads = lambda ref, t: jnp.concatenate([ref[p, t, :] for p in range(N_PAIRS)], axis=-1)
    xs = []
    for t in tiles:
        a = _rms(heads(na_ref, t), g_na_ref[...]).astype(BF16)
        d = _rms(heads(dil_ref, t), g_dil_ref[...]).astype(BF16)
        mixed = (jnp.dot(a, w_ref[:D_NA, :], preferred_element_type=F32)
                 + jnp.dot(d, w_ref[D_NA:, :], preferred_element_type=F32))
        xs.append(x_ref[t, :] + _rms(mixed, g_mix_ref[...]))
    for t, out in zip(tiles, _half_step_ffn(xs, *refs[:-1])):
        o_ref[t, :] = out


def _ffn_specs():
    return [_resident((1, D_MODEL)), _resident((D_MODEL, D_FF)), _resident((D_MODEL, D_FF)),
            _resident((D_FF, D_MODEL)), _resident((1, D_MODEL))]


def _ffn(x, ffn_params, *, tm=512):
    b, s, _ = x.shape
    row = pl.BlockSpec((None, tm, D_MODEL), lambda bi, i: (bi, i, 0))
    return pl.pallas_call(
        _ffn_kernel,
        grid=(b, s // tm),
        in_specs=[row] + _ffn_specs(),
        out_specs=row,
        out_shape=jax.ShapeDtypeStruct((b, s, D_MODEL), F32),
        compiler_params=pltpu.CompilerParams(
            dimension_semantics=("arbitrary", "arbitrary"), vmem_limit_bytes=VMEM_LIMIT),
        name="ffn",
    )(x, *ffn_params)


def _mix_ffn(x, o_na, o_dil, g_na, g_dil, w, g_mix, ffn_params, *, tm=512):
    b, s, _ = x.shape
    row = pl.BlockSpec((None, tm, D_MODEL), lambda bi, i: (bi, i, 0))
    heads = pl.BlockSpec((None, N_PAIRS, tm, LANES), lambda bi, i: (bi, 0, i, 0))
    return pl.pallas_call(
        _mix_ffn_kernel,
        grid=(b, s // tm),
        in_specs=[row, heads, heads, _resident((1, D_NA)), _resident((1, D_DIL)),
                  _resident((D_NA + D_DIL, D_MODEL)), _resident((1, D_MODEL))] + _ffn_specs(),
        out_specs=row,
        out_shape=jax.ShapeDtypeStruct((b, s, D_MODEL), F32),
        compiler_params=pltpu.CompilerParams(
            dimension_semantics=("arbitrary", "arbitrary"), vmem_limit_bytes=VMEM_LIMIT),
        name="mix_ffn",
    )(x, o_na, o_dil, g_na, g_dil, w, g_mix, *ffn_params)


def _qkv_kernel(x_ref, g_ref, w_ref, na_ref, d1_ref, d4_ref, d16_ref, scr_ref, scr4_ref, *, tm):
    h = _rms(x_ref[0], g_ref[...]).astype(BF16)
    for c in range(0, D_GROUP, MXU_COLS):
        y = jnp.dot(h, w_ref[:, D_GROUP + c:D_GROUP + c + MXU_COLS], preferred_element_type=F32)
        for lo in range(0, MXU_COLS, LANES):
            t = (c + lo) // LANES
            y_t = y[:, lo:lo + LANES]
            if t >= N_PAIRS:
                d1_ref[0, t - N_PAIRS] = y_t.astype(BF16)
            scr_ref[t] = y_t
            for r in range(4):
                cls = scr_ref[t, pl.ds(r, tm // 4, stride=4), :]
                d4_ref[0, t, r] = cls.astype(BF16)
                scr4_ref[t, r] = cls
                for r2 in range(4):
                    d16_ref[0, t, 4 * r2 + r] = (
                        scr4_ref[t, r, pl.ds(r2, tm // 16, stride=4), :].astype(BF16))
    for c in range(0, D_GROUP, MXU_COLS):
        y = jnp.dot(h, w_ref[:, c:c + MXU_COLS], preferred_element_type=F32)
        for lo in range(0, MXU_COLS, LANES):
            na_ref[0, (c + lo) // LANES] = y[:, lo:lo + LANES].astype(BF16)


def _qkv(x, g, w, *, tm=512):
    b, s, _ = x.shape
    n_t = D_GROUP // LANES
    return pl.pallas_call(
        functools.partial(_qkv_kernel, tm=tm),
        grid=(b, s // tm),
        in_specs=[
            pl.BlockSpec((1, tm, D_MODEL), lambda bi, i: (bi, i, 0)),
            _resident((1, D_MODEL)),
            _resident((D_MODEL, 2 * D_GROUP)),
        ],
        out_specs=[
            pl.BlockSpec((1, n_t, tm, LANES), lambda bi, i: (bi, 0, i, 0)),
            pl.BlockSpec((1, n_t - N_PAIRS, tm, LANES), lambda bi, i: (bi, 0, i, 0)),
            pl.BlockSpec((1, n_t, 4, tm // 4, LANES), lambda bi, i: (bi, 0, 0, i, 0)),
            pl.BlockSpec((1, n_t, 16, tm // 16, LANES), lambda bi, i: (bi, 0, 0, i, 0)),
        ],
        out_shape=[
            jax.ShapeDtypeStruct((b, n_t, s, LANES), BF16),
            jax.ShapeDtypeStruct((b, n_t - N_PAIRS, s, LANES), BF16),
            jax.ShapeDtypeStruct((b, n_t, 4, s // 4, LANES), BF16),
            jax.ShapeDtypeStruct((b, n_t, 16, s // 16, LANES), BF16),
        ],
        scratch_shapes=[pltpu.VMEM((n_t, tm, LANES), F32),
                        pltpu.VMEM((n_t, 4, tm // 4, LANES), F32)],
        compiler_params=pltpu.CompilerParams(
            dimension_semantics=("arbitrary", "arbitrary"), vmem_limit_bytes=VMEM_LIMIT),
        name="qkv",
    )(x, g, w)


def _attend_blocks(blocks):
    lane = lax.broadcasted_iota(jnp.int32, (1, LANES), 1)
    second = lane >= HEAD_DIM
    scores = []
    for q, k, _, table in blocks:
        zero = jnp.zeros_like(q)
        q2 = jnp.concatenate([jnp.where(second, zero, q), jnp.where(second, q, zero)], axis=0)
        scores.append(lax.dot_general(q2, k, (((1,), (1,)), ((), ())),
                                      preferred_element_type=F32) + table)
    probs = []
    for s in scores:
        m = jnp.max(s, axis=-1, keepdims=True)
        p = jnp.exp(s - m)
        probs.append((m, jnp.sum(p, axis=-1, keepdims=True), p.astype(BF16)))
    out = []
    for (q, _, v, _), (m, l, p) in zip(blocks, probs):
        m_rows = q.shape[0]
        pv = jnp.dot(p, v, preferred_element_type=F32)
        pick = lambda a, m_rows=m_rows: jnp.where(second, a[m_rows:], a[:m_rows])
        out.append((pick(m), pick(l), pick(pv)))
    return out


def _toeplitz_rows(base_row, n_rows, first=0, step=1):
    base = jnp.broadcast_to(base_row, (n_rows, base_row.shape[-1]))
    return pltpu.roll(base, first, 1, stride=step, stride_axis=0)


def _na_kernel(q_ref, k_ref, v_ref, base_ref, o_ref, tbl_ref, *, unroll):
    rows = q_ref.shape[0] // GRID_W
    n_keys = NA_KH * GRID_W

    @pl.when(pl.program_id(1) == 0)
    def _():
        qc = lax.broadcasted_iota(jnp.int32, (GRID_W, LANES), 0)
        kc = lax.broadcasted_iota(jnp.int32, (GRID_W, LANES), 1) % GRID_W
        qs = jnp.clip(qc - NA_KW // 2, 0, GRID_W - NA_KW)
        valid = (kc >= qs) & (kc < qs + NA_KW)
        for h in range(HEADS_PER_TILE):
            for dr in range(2 * NA_KH - 2):
                tile = jnp.where(valid, _toeplitz_rows(base_ref[dr, h:h + 1, :], GRID_W), -jnp.inf)
                for t in range(n_keys // LANES):
                    v = NA_KH - 1 - dr + 2 * t
                    if 0 <= v < NA_KH:
                        tbl_ref[v, h * GRID_W:(h + 1) * GRID_W, t * LANES:(t + 1) * LANES] = tile

    def step(i, carry):
        blocks, starts = [], []
        for u in range(unroll):
            r = i * unroll + u
            kr0 = jnp.clip(r - NA_KH // 2, 0, rows - NA_KH)
            q0 = pl.multiple_of(r * GRID_W, GRID_W)
            k0 = pl.multiple_of(kr0 * GRID_W, GRID_W)
            starts.append(q0)
            blocks.append((q_ref[pl.ds(q0, GRID_W), :], k_ref[pl.ds(k0, n_keys), :],
                           v_ref[pl.ds(k0, n_keys), :], tbl_ref[r - kr0]))
        for q0, (_, l_t, acc_t) in zip(starts, _attend_blocks(blocks)):
            o_ref[pl.ds(q0, GRID_W), :] = acc_t / l_t
        return carry

    lax.fori_loop(0, rows // unroll, step, 0)


def _na(qkv_na, bases, *, unroll=64):
    b, _, s, _ = qkv_na.shape
    blk = lambda off: pl.BlockSpec((None, None, s, LANES), lambda p, bi: (bi, off + p, 0, 0))
    return pl.pallas_call(
        functools.partial(_na_kernel, unroll=unroll),
        grid=(N_PAIRS, b),
        in_specs=[
            blk(0), blk(N_PAIRS), blk(2 * N_PAIRS),
            pl.BlockSpec((None,) + bases.shape[1:], lambda p, bi: (p, 0, 0, 0)),
        ],
        out_specs=pl.BlockSpec((None, None, s, LANES), lambda p, bi: (bi, p, 0, 0)),
        out_shape=jax.ShapeDtypeStruct((b, N_PAIRS, s, LANES), F32),
        scratch_shapes=[pltpu.VMEM((NA_KH, HEADS_PER_TILE * GRID_W, NA_KH * GRID_W), F32)],
        compiler_params=pltpu.CompilerParams(
            dimension_semantics=("arbitrary", "arbitrary"), vmem_limit_bytes=VMEM_LIMIT),
        name="na_attn",
    )(qkv_na, qkv_na, qkv_na, bases)


def _dil_kernel(k1_ref, v1_ref, q4_ref, k4_ref, v4_ref, q16_ref, k16_ref, v16_ref,
                base_ref, o_ref, m_ref, l_ref, acc_ref, tbl_ref, *, unroll):
    s_len = k1_ref.shape[0]
    qb = DIL_QBLOCK
    kb = DIL_KBLOCK
    mid = DIL_PATTERNS[1][1]
    sub = qb // mid
    state = (m_ref, l_ref, acc_ref)

    @pl.when(pl.program_id(1) == 0)
    def _():
        for pat in range(len(DIL_PATTERNS)):
            for var in range(DIL_VARIANTS):
                for h in range(HEADS_PER_TILE):
                    base = base_ref[pat, var, h:h + 1, :]
                    if pat == 0:
                        for a in range(mid):
                            lo = h * qb + a * sub
                            tbl_ref[pat, var, lo:lo + sub, :] = (
                                _toeplitz_rows(base, sub, first=a, step=mid)[:, :kb])
                    else:
                        tbl_ref[pat, var, h * qb:(h + 1) * qb, :] = _toeplitz_rows(base, qb)[:, :kb]

    def window(i, n_blocks):
        q0 = pl.multiple_of(i * qb, qb)
        w0 = pl.multiple_of(jnp.clip(q0 - DIL_RADIUS, 0, n_blocks * qb - kb), DIL_RADIUS)
        var = jnp.where(i == 0, 0, jnp.where(i == n_blocks - 1, DIL_VARIANTS - 1, 1))
        return q0, w0, var

    def merge(old, new):
        m_new = jnp.maximum(old[0], new[0])
        a = jnp.exp(old[0] - m_new)
        bb = jnp.exp(new[0] - m_new)
        return m_new, a * old[1] + bb * new[1], a * old[2] + bb * new[2]

    def de_interleaved(pat, dil, q_ref, k_ref, v_ref):
        nb = s_len // dil // qb
        first = dil > mid

        def body(j, carry):
            blocks, dests = [], []
            for u in range(unroll):
                idx = j * unroll + u
                c = idx // nb
                q0, w0, var = window(idx % nb, nb)
                blocks.append((q_ref[c, pl.ds(q0, qb), :], k_ref[c, pl.ds(w0, kb), :],
                               v_ref[c, pl.ds(w0, kb), :], tbl_ref[pat, var]))
                step = dil // mid
                dests.append((c % mid, pl.ds(c // mid + q0 * step, qb, stride=step) if first
                              else pl.ds(q0, qb)))
            for (r, rows), pieces in zip(dests, _attend_blocks(blocks)):
                if not first:
                    pieces = merge([ref[r, rows, :] for ref in state], pieces)
                for ref, piece in zip(state, pieces):
                    ref[r, rows, :] = piece
            return carry

        lax.fori_loop(0, dil * nb // unroll, body, 0)

    de_interleaved(2, DIL_PATTERNS[2][1], q16_ref, k16_ref, v16_ref)
    de_interleaved(1, mid, q4_ref, k4_ref, v4_ref)

    nb1 = s_len // qb
    gather = lambda ref, p0: jnp.concatenate(
        [ref[a, pl.ds(p0, sub), :] for a in range(mid)], axis=0)

    def p1(j, carry):
        blocks, starts = [], []
        for u in range(unroll):
            i = j * unroll + u
            q0, w0, var = window(i, nb1)
            p0 = pl.multiple_of(i * sub, sub)
            starts.append((q0, p0))
            blocks.append((gather(q4_ref, p0), k1_ref[pl.ds(w0, kb), :],
                           v1_ref[pl.ds(w0, kb), :], tbl_ref[0, var]))
        for (q0, p0), pieces in zip(starts, _attend_blocks(blocks)):
            _, l_n, acc_n = merge([gather(ref, p0) for ref in state], pieces)
            out = acc_n / l_n
            for a in range(mid):
                o_ref[pl.ds(q0 + a, sub, stride=mid), :] = out[a * sub:(a + 1) * sub]
        return carry

    lax.fori_loop(0, nb1 // unroll, p1, 0)


def _dil(d1, d4, d16, bases, *, unroll=32):
    b, _, s, _ = d1.shape
    mid = DIL_PATTERNS[1][1]
    blk1 = lambda off: pl.BlockSpec((None, None, s, LANES), lambda p, bi: (bi, off + p, 0, 0))
    blkd = lambda d, off: pl.BlockSpec((None, None, d, s // d, LANES),
                                       lambda p, bi: (bi, off + p, 0, 0, 0))
    specs = [blk1(0), blk1(N_PAIRS)]
    for d in (mid, DIL_PATTERNS[2][1]):
        specs += [blkd(d, 0), blkd(d, N_PAIRS), blkd(d, 2 * N_PAIRS)]
    specs.append(pl.BlockSpec((None,) + bases.shape[1:], lambda p, bi: (p, 0, 0, 0, 0)))
    n_pat, n_var = bases.shape[1:3]
    return pl.pallas_call(
        functools.partial(_dil_kernel, unroll=unroll),
        grid=(N_PAIRS, b),
        in_specs=specs,
        out_specs=pl.BlockSpec((None, None, s, LANES), lambda p, bi: (bi, p, 0, 0)),
        out_shape=jax.ShapeDtypeStruct((b, N_PAIRS, s, LANES), F32),
        scratch_shapes=[pltpu.VMEM((mid, s // mid, LANES), F32)] * 3
        + [pltpu.VMEM((n_pat, n_var, HEADS_PER_TILE * DIL_QBLOCK, DIL_KBLOCK), F32)],
        compiler_params=pltpu.CompilerParams(
            dimension_semantics=("arbitrary", "arbitrary"), vmem_limit_bytes=VMEM_LIMIT),
        name="dil_attn",
    )(d1, d1, d4, d4, d4, d16, d16, d16, bases)


def _by_pair(bases):
    t = bases.reshape((N_PAIRS, HEADS_PER_TILE) + bases.shape[1:])
    return jnp.moveaxis(t, 1, -2)


def _t5_bucket(rel):
    nb = T5_BUCKETS // 2
    max_exact = nb // 2
    n = jnp.abs(rel)
    large = max_exact + (jnp.log(jnp.maximum(n, 1).astype(F32) / max_exact)
                         / math.log(T5_MAX_DIST / max_exact) * (nb - max_exact)).astype(jnp.int32)
    large = jnp.minimum(large, nb - 1)
    return jnp.where(rel > 0, nb, 0) + jnp.where(n < max_exact, n, large)


def _na_bases(rel_bias):
    rb = rel_bias.astype(F32)
    gap = jnp.full(rb.shape[:1] + (2 * NA_KH - 2, GRID_W - 2 * NA_KW + 1), -jnp.inf, F32)
    bases = jnp.concatenate(
        [rb[:, :-1, NA_KW - 1:], gap, rb[:, 1:, :], gap, rb[:, :-1, :NA_KW - 1]], axis=-1)
    return _by_pair(bases)


def _dil_bases(t5_bias):
    span = 2 * DIL_RADIUS + 1
    off = np.arange(span) - DIL_RADIUS
    dils = np.asarray([d for _, d in DIL_PATTERNS])
    bucket = _t5_bucket(jnp.asarray(off[None, :] * dils[:, None], jnp.int32))
    vec = jnp.take(t5_bias.astype(F32), bucket, axis=1)
    width = 2 * DIL_KBLOCK
    per_var = []
    for v in range(DIL_VARIANTS):
        lo = DIL_RADIUS * (1 - v)
        pos = jnp.pad(vec[..., max(lo, 0):], ((0, 0), (0, 0), (max(-lo, 0), 0)),
                      constant_values=-jnp.inf)
        pos = jnp.pad(pos, ((0, 0), (0, 0), (0, width - pos.shape[-1] - max(lo, 0))),
                      constant_values=-jnp.inf)
        per_var.append(jnp.concatenate([pos, vec[..., :max(lo, 0)]], axis=-1))
    return _by_pair(jnp.stack(per_var, axis=2))


def _cast_kernel(*refs):
    n = len(refs) // 2
    for src, dst in zip(refs[:n], refs[n:]):
        dst[...] = src[...].astype(BF16)


def _cast_bf16(arrays, *, n_steps=8):
    specs = [pl.BlockSpec((a.shape[0] // n_steps, a.shape[1]), lambda i: (i, 0)) for a in arrays]
    return pl.pallas_call(
        _cast_kernel,
        grid=(n_steps,),
        in_specs=specs,
        out_specs=specs,
        out_shape=[jax.ShapeDtypeStruct(a.shape, BF16) for a in arrays],
        compiler_params=pltpu.CompilerParams(
            dimension_semantics=("arbitrary",), vmem_limit_bytes=VMEM_LIMIT),
        name="cast_weights",
    )(*arrays)


def _group_columns_kernel(w_ref, o_ref):
    is_q = pl.program_id(0) % 3 == 0
    o_ref[...] = (w_ref[...] * jnp.where(is_q, HEAD_DIM ** -0.5, 1.0)).astype(BF16)


def _group_columns(w_qkv):
    return pl.pallas_call(
        _group_columns_kernel,
        grid=(2 * 3,),
        in_specs=[pl.BlockSpec((D_MODEL, D_NA), lambda j: (0, (j % 3) * 2 + j // 3))],
        out_specs=pl.BlockSpec((D_MODEL, D_NA), lambda j: (0, j)),
        out_shape=jax.ShapeDtypeStruct((D_MODEL, 2 * D_GROUP), BF16),
        compiler_params=pltpu.CompilerParams(
            dimension_semantics=("arbitrary",), vmem_limit_bytes=VMEM_LIMIT),
        name="group_qkv_columns",
    )(w_qkv)


def kernel(x, ffn1_pre_g, ffn1_w_gate, ffn1_w_up, ffn1_w_down, ffn1_post_g, mix_pre_g, w_qkv,
           na_rel_bias, t5_rel_bias, na_out_g, dil_out_g, w_out, mix_post_g, ffn2_pre_g,
           ffn2_w_gate, ffn2_w_up, ffn2_w_down, ffn2_post_g):
    b, s, d = x.shape
    depth = ffn1_pre_g.shape[0]
    dil_bases = _dil_bases(t5_rel_bias)
    for l in range(depth):
        wg1, wu1, wd1, wg2, wu2, wd2, wo = _cast_bf16(
            [ffn1_w_gate[l], ffn1_w_up[l], ffn1_w_down[l],
             ffn2_w_gate[l], ffn2_w_up[l], ffn2_w_down[l], w_out[l]])
        x = _ffn(x, (ffn1_pre_g[l][None], wg1, wu1, wd1, ffn1_post_g[l][None]))
        qkv_na, d1, d4, d16 = _qkv(x, mix_pre_g[l][None], _group_columns(w_qkv[l]))
        o_na = _na(qkv_na, _na_bases(na_rel_bias[l]))
        o_dil = _dil(d1, d4, d16, dil_bases)
        x = _mix_ffn(x, o_na, o_dil, na_out_g[l][None], dil_out_g[l][None], wo,
                     mix_post_g[l][None],
                     (ffn2_pre_g[l][None], wg2, wu2, wd2, ffn2_post_g[l][None]))
    return x
```

```python
import functools
import math

import jax
import jax.numpy as jnp
import numpy as np
from jax import lax
from jax.experimental import pallas as pl
from jax.experimental.pallas import tpu as pltpu

D_MODEL = 1024
HEAD_DIM = 64
N_HEADS_NA = 8
N_HEADS_DIL = 8
D_NA = N_HEADS_NA * HEAD_DIM
D_DIL = N_HEADS_DIL * HEAD_DIM
D_FF = 2816
GRID_W = 64
NA_KH = 8
NA_KW = 16
DIL_PATTERNS = ((128, 1), (512, 4), (2048, 16))
DIL_QBLOCK = 128
DIL_RADIUS = 64
DIL_KBLOCK = DIL_QBLOCK + 2 * DIL_RADIUS
DIL_VARIANTS = 3
DIL_MID, DIL_FAR = DIL_PATTERNS[1][1], DIL_PATTERNS[2][1]
assert DIL_PATTERNS[0][1] == 1 and DIL_FAR == DIL_MID ** 2
T5_BUCKETS = 32
T5_MAX_DIST = 1024
NORM_EPS = 1e-6

LANES = 128
MXU_COLS = 256
FFN_ROWS = 128
HEADS_PER_TILE = LANES // HEAD_DIM
N_PAIRS = N_HEADS_NA // HEADS_PER_TILE
D_GROUP = 3 * D_NA
VMEM_LIMIT = 56 * 1024 * 1024

BF16 = jnp.bfloat16
F32 = jnp.float32


def _rms(x, g):
    return x * lax.rsqrt(jnp.mean(x * x, axis=-1, keepdims=True) + NORM_EPS) * g


def _resident(shape):
    return pl.BlockSpec(shape, lambda *_: (0,) * len(shape), pipeline_mode=pl.Buffered(1))


def _half_step_ffn(xs, pre_g_ref, wg_ref, wu_ref, wd_ref, post_g_ref):
    hs = [_rms(x, pre_g_ref[...]).astype(BF16) for x in xs]
    gus = [(jnp.dot(h, wg_ref[...], preferred_element_type=F32),
            jnp.dot(h, wu_ref[...], preferred_element_type=F32)) for h in hs]
    acts = [(g * jax.nn.sigmoid(g) * u).astype(BF16) for g, u in gus]
    ys = [jnp.dot(a, wd_ref[...], preferred_element_type=F32) for a in acts]
    return [x + 0.5 * _rms(y, post_g_ref[...]) for x, y in zip(xs, ys)]


def _row_tiles(n_rows):
    return [slice(r, r + FFN_ROWS) for r in range(0, n_rows, FFN_ROWS)]


def _ffn_kernel(x_ref, *refs):
    o_ref = refs[-1]
    tiles = _row_tiles(x_ref.shape[0])
    for t, out in zip(tiles, _half_step_ffn([x_ref[t, :] for t in tiles], *refs[:-1])):
        o_ref[t, :] = out


def _mix_ffn_kernel(x_ref, na_ref, dil_ref, g_na_ref, g_dil_ref, w_ref, g_mix_ref, *refs):
    o_ref = refs[-1]
    tiles = _row_tiles(x_ref.shape[0])
    heads = lambda ref, t: jnp.concatenate([ref[p, t, :] for p in range(N_PAIRS)], axis=-1)
    xs = []
    for t in tiles:
        a = _rms(heads(na_ref, t), g_na_ref[...]).astype(BF16)
        d = _rms(heads(dil_ref, t), g_dil_ref[...]).astype(BF16)
        mixed = (jnp.dot(a, w_ref[:D_NA, :], preferred_element_type=F32)
                 + jnp.dot(d, w_ref[D_NA:, :], preferred_element_type=F32))
        xs.append(x_ref[t, :] + _rms(mixed, g_mix_ref[...]))
    for t, out in zip(tiles, _half_step_ffn(xs, *refs[:-1])):
        o_ref[t, :] = out


def _ffn_specs():
    return [_resident((1, D_MODEL)), _resident((D_MODEL, D_FF)), _resident((D_MODEL, D_FF)),
            _resident((D_FF, D_MODEL)), _resident((1, D_MODEL))]


def _ffn(x, ffn_params, *, tm=1024):
    b, s, _ = x.shape
    row = pl.BlockSpec((None, tm, D_MODEL), lambda bi, i: (bi, i, 0))
    return pl.pallas_call(
        _ffn_kernel,
        grid=(b, s // tm),
        in_specs=[row] + _ffn_specs(),
        out_specs=row,
        out_shape=jax.ShapeDtypeStruct((b, s, D_MODEL), F32),
        compiler_params=pltpu.CompilerParams(
            dimension_semantics=("arbitrary", "arbitrary"), vmem_limit_bytes=VMEM_LIMIT),
        name="ffn",
    )(x, *ffn_params)


def _mix_ffn(x, o_na, o_dil, g_na, g_dil, w, g_mix, ffn_params, *, tm=512):
    b, s, _ = x.shape
    row = pl.BlockSpec((None, tm, D_MODEL), lambda bi, i: (bi, i, 0))
    heads = pl.BlockSpec((None, N_PAIRS, tm, LANES), lambda bi, i: (bi, 0, i, 0))
    return pl.pallas_call(
        _mix_ffn_kernel,
        grid=(b, s // tm),
        in_specs=[row, heads, heads, _resident((1, D_NA)), _resident((1, D_DIL)),
                  _resident((D_NA + D_DIL, D_MODEL)), _resident((1, D_MODEL))] + _ffn_specs(),
        out_specs=row,
        out_shape=jax.ShapeDtypeStruct((b, s, D_MODEL), F32),
        compiler_params=pltpu.CompilerParams(
            dimension_semantics=("arbitrary", "arbitrary"), vmem_limit_bytes=VMEM_LIMIT),
        name="mix_ffn",
    )(x, o_na, o_dil, g_na, g_dil, w, g_mix, *ffn_params)


def _qkv_kernel(x_ref, g_ref, w_ref, na_ref, d1_ref, d4_ref, d16_ref, scr_ref, scr4_ref, *, tm):
    h = _rms(x_ref[0], g_ref[...]).astype(BF16)
    for c in range(0, D_GROUP, MXU_COLS):
        y = jnp.dot(h, w_ref[:, D_GROUP + c:D_GROUP + c + MXU_COLS], preferred_element_type=F32)
        for lo in range(0, MXU_COLS, LANES):
            t = (c + lo) // LANES
            y_t = y[:, lo:lo + LANES]
            if t >= N_PAIRS:
                d1_ref[0, t - N_PAIRS] = y_t.astype(BF16)
            scr_ref[t] = y_t
            for r in range(DIL_MID):
                cls = scr_ref[t, pl.ds(r, tm // DIL_MID, stride=DIL_MID), :]
                d4_ref[0, t, r] = cls.astype(BF16)
                scr4_ref[t, r] = cls
                for r2 in range(DIL_MID):
                    d16_ref[0, t, DIL_MID * r2 + r] = (
                        scr4_ref[t, r, pl.ds(r2, tm // DIL_FAR, stride=DIL_MID), :].astype(BF16))
        y = jnp.dot(h, w_ref[:, c:c + MXU_COLS], preferred_element_type=F32)
        for lo in range(0, MXU_COLS, LANES):
            na_ref[0, (c + lo) // LANES] = y[:, lo:lo + LANES].astype(BF16)


def _qkv(x, g, w, *, tm=512):
    b, s, _ = x.shape
    n_t = D_GROUP // LANES
    return pl.pallas_call(
        functools.partial(_qkv_kernel, tm=tm),
        grid=(b, s // tm),
        in_specs=[
            pl.BlockSpec((1, tm, D_MODEL), lambda bi, i: (bi, i, 0)),
            _resident((1, D_MODEL)),
            _resident((D_MODEL, 2 * D_GROUP)),
        ],
        out_specs=[
            pl.BlockSpec((1, n_t, tm, LANES), lambda bi, i: (bi, 0, i, 0)),
            pl.BlockSpec((1, n_t - N_PAIRS, tm, LANES), lambda bi, i: (bi, 0, i, 0)),
            pl.BlockSpec((1, n_t, DIL_MID, tm // DIL_MID, LANES), lambda bi, i: (bi, 0, 0, i, 0)),
            pl.BlockSpec((1, n_t, DIL_FAR, tm // DIL_FAR, LANES), lambda bi, i: (bi, 0, 0, i, 0)),
        ],
        out_shape=[
            jax.ShapeDtypeStruct((b, n_t, s, LANES), BF16),
            jax.ShapeDtypeStruct((b, n_t - N_PAIRS, s, LANES), BF16),
            jax.ShapeDtypeStruct((b, n_t, DIL_MID, s // DIL_MID, LANES), BF16),
            jax.ShapeDtypeStruct((b, n_t, DIL_FAR, s // DIL_FAR, LANES), BF16),
        ],
        scratch_shapes=[pltpu.VMEM((n_t, tm, LANES), F32),
                        pltpu.VMEM((n_t, DIL_MID, tm // DIL_MID, LANES), F32)],
        compiler_params=pltpu.CompilerParams(
            dimension_semantics=("arbitrary", "arbitrary"), vmem_limit_bytes=VMEM_LIMIT),
        name="qkv",
    )(x, g, w)


def _attend_blocks(blocks):
    lane = lax.broadcasted_iota(jnp.int32, (1, LANES), 1)
    second = lane >= HEAD_DIM
    scores = []
    for q, k, _, table in blocks:
        zero = jnp.zeros_like(q)
        q2 = jnp.concatenate([jnp.where(second, zero, q), jnp.where(second, q, zero)], axis=0)
        scores.append(lax.dot_general(q2, k, (((1,), (1,)), ((), ())),
                                      preferred_element_type=F32) + table)
    probs = []
    for s in scores:
        m = jnp.max(s, axis=-1, keepdims=True)
        p = jnp.exp(s - m)
        probs.append((m, jnp.sum(p, axis=-1, keepdims=True), p.astype(BF16)))
    out = []
    for (q, _, v, _), (m, l, p) in zip(blocks, probs):
        m_rows = q.shape[0]
        pv = jnp.dot(p, v, preferred_element_type=F32)
        pick = lambda a, m_rows=m_rows: jnp.where(second, a[m_rows:], a[:m_rows])
        out.append((pick(m), pick(l), pick(pv)))
    return out


def _toeplitz_rows(base_row, n_rows, first=0, step=1):
    base = jnp.broadcast_to(base_row, (n_rows, base_row.shape[-1]))
    return pltpu.roll(base, first, 1, stride=step, stride_axis=0)


def _na_kernel(q_ref, k_ref, v_ref, base_ref, o_ref, tbl_ref):
    rows = q_ref.shape[0] // GRID_W
    n_keys = NA_KH * GRID_W

    @pl.when(pl.program_id(1) == 0)
    def _():
        qc = lax.broadcasted_iota(jnp.int32, (GRID_W, LANES), 0)
        kc = lax.broadcasted_iota(jnp.int32, (GRID_W, LANES), 1) % GRID_W
        qs = jnp.clip(qc - NA_KW // 2, 0, GRID_W - NA_KW)
        valid = (kc >= qs) & (kc < qs + NA_KW)
        for h in range(HEADS_PER_TILE):
            for dr in range(2 * NA_KH - 2):
                tile = jnp.where(valid, _toeplitz_rows(base_ref[dr, h:h + 1, :], GRID_W), -jnp.inf)
                for t in range(n_keys // LANES):
                    v = NA_KH - 1 - dr + 2 * t
                    if 0 <= v < NA_KH:
                        tbl_ref[v, h * GRID_W:(h + 1) * GRID_W, t * LANES:(t + 1) * LANES] = tile

    blocks = []
    for r in range(rows):
        kr0 = min(max(r - NA_KH // 2, 0), rows - NA_KH)
        keys = pl.ds(kr0 * GRID_W, n_keys)
        blocks.append((q_ref[pl.ds(r * GRID_W, GRID_W), :], k_ref[keys, :], v_ref[keys, :],
                       tbl_ref[r - kr0]))
    for r, (_, l_t, acc_t) in enumerate(_attend_blocks(blocks)):
        o_ref[pl.ds(r * GRID_W, GRID_W), :] = acc_t / l_t


def _na(qkv_na, bases):
    b, _, s, _ = qkv_na.shape
    blk = lambda off: pl.BlockSpec((None, None, s, LANES), lambda p, bi: (bi, off + p, 0, 0))
    return pl.pallas_call(
        _na_kernel,
        grid=(N_PAIRS, b),
        in_specs=[
            blk(0), blk(N_PAIRS), blk(2 * N_PAIRS),
            pl.BlockSpec((None,) + bases.shape[1:], lambda p, bi: (p, 0, 0, 0)),
        ],
        out_specs=pl.BlockSpec((None, None, s, LANES), lambda p, bi: (bi, p, 0, 0)),
        out_shape=jax.ShapeDtypeStruct((b, N_PAIRS, s, LANES), F32),
        scratch_shapes=[pltpu.VMEM((NA_KH, HEADS_PER_TILE * GRID_W, NA_KH * GRID_W), F32)],
        compiler_params=pltpu.CompilerParams(
            dimension_semantics=("arbitrary", "arbitrary"), vmem_limit_bytes=VMEM_LIMIT),
        name="na_attn",
    )(qkv_na, qkv_na, qkv_na, bases)


def _dil_kernel(k1_ref, v1_ref, q4_ref, k4_ref, v4_ref, q16_ref, k16_ref, v16_ref,
                base_ref, o_ref, m_ref, l_ref, acc_ref, tbl_ref):
    s_len = k1_ref.shape[0]
    qb = DIL_QBLOCK
    kb = DIL_KBLOCK
    mid = DIL_MID
    sub = qb // mid
    state = (m_ref, l_ref, acc_ref)

    @pl.when(pl.program_id(1) == 0)
    def _():
        for pat in range(len(DIL_PATTERNS)):
            for var in range(DIL_VARIANTS):
                for h in range(HEADS_PER_TILE):
                    base = base_ref[pat, var, h:h + 1, :]
                    if pat == 0:
                        for a in range(mid):
                            lo = h * qb + a * sub
                            tbl_ref[pat, var, lo:lo + sub, :] = (
                                _toeplitz_rows(base, sub, first=a, step=mid)[:, :kb])
                    else:
                        tbl_ref[pat, var, h * qb:(h + 1) * qb, :] = _toeplitz_rows(base, qb)[:, :kb]

    def merge(old, new):
        m_new = jnp.maximum(old[0], new[0])
        a = jnp.exp(old[0] - m_new)
        bb = jnp.exp(new[0] - m_new)
        return m_new, a * old[1] + bb * new[1], a * old[2] + bb * new[2]


    def window(i, n_blocks):
        q0 = i * qb
        w0 = min(max(q0 - DIL_RADIUS, 0), n_blocks * qb - kb)
        var = 0 if i == 0 else (DIL_VARIANTS - 1 if i == n_blocks - 1 else 1)
        return q0, w0, var

    def de_interleaved(pat, dil, q_ref, k_ref, v_ref):
        nb = s_len // dil // qb
        first = dil > mid
        step = dil // mid
        blocks, finishers = [], []
        for idx in range(dil * nb):
            c = idx // nb
            q0, w0, var = window(idx % nb, nb)
            blocks.append((q_ref[c, pl.ds(q0, qb), :], k_ref[c, pl.ds(w0, kb), :],
                           v_ref[c, pl.ds(w0, kb), :], tbl_ref[pat, var]))
            r = c % mid
            rows = pl.ds(c // mid + q0 * step, qb, stride=step) if first else pl.ds(q0, qb)

            def finish(pieces, r=r, rows=rows):
                if not first:
                    pieces = merge([ref[r, rows, :] for ref in state], pieces)
                for ref, piece in zip(state, pieces):
                    ref[r, rows, :] = piece

            finishers.append(finish)
        return blocks, finishers

    def token_order():
        nb1 = s_len // qb
        gather = lambda ref, p0: jnp.concatenate(
            [ref[a, pl.ds(p0, sub), :] for a in range(mid)], axis=0)
        blocks, finishers = [], []
        for i in range(nb1):
            q0, w0, var = window(i, nb1)
            p0 = i * sub
            blocks.append((gather(q4_ref, p0), k1_ref[pl.ds(w0, kb), :],
                           v1_ref[pl.ds(w0, kb), :], tbl_ref[0, var]))

            def finish(pieces, q0=q0, p0=p0):
                _, l_n, acc_n = merge([gather(ref, p0) for ref in state], pieces)
                out = acc_n / l_n
                for a in range(mid):
                    o_ref[pl.ds(q0 + a, sub, stride=mid), :] = out[a * sub:(a + 1) * sub]

            finishers.append(finish)
        return blocks, finishers

    def run(*phases):
        blocks = [blk for phase_blocks, _ in phases for blk in phase_blocks]
        finishers = [fin for _, phase_finishers in phases for fin in phase_finishers]
        for fin, pieces in zip(finishers, _attend_blocks(blocks)):
            fin(pieces)

    run(de_interleaved(2, DIL_FAR, q16_ref, k16_ref, v16_ref))
    run(de_interleaved(1, mid, q4_ref, k4_ref, v4_ref))
    run(token_order())


def _dil(d1, d4, d16, bases):
    b, _, s, _ = d1.shape
    mid = DIL_MID
    blk1 = lambda off: pl.BlockSpec((None, None, s, LANES), lambda p, bi: (bi, off + p, 0, 0))
    blkd = lambda d, off: pl.BlockSpec((None, None, d, s // d, LANES),
                                       lambda p, bi: (bi, off + p, 0, 0, 0))
    specs = [blk1(0), blk1(N_PAIRS)]
    for d in (mid, DIL_FAR):
        specs += [blkd(d, 0), blkd(d, N_PAIRS), blkd(d, 2 * N_PAIRS)]
    specs.append(pl.BlockSpec((None,) + bases.shape[1:], lambda p, bi: (p, 0, 0, 0, 0)))
    n_pat, n_var = bases.shape[1:3]
    return pl.pallas_call(
        _dil_kernel,
        grid=(N_PAIRS, b),
        in_specs=specs,
        out_specs=pl.BlockSpec((None, None, s, LANES), lambda p, bi: (bi, p, 0, 0)),
        out_shape=jax.ShapeDtypeStruct((b, N_PAIRS, s, LANES), F32),
        scratch_shapes=[pltpu.VMEM((mid, s // mid, LANES), F32)] * 3
        + [pltpu.VMEM((n_pat, n_var, HEADS_PER_TILE * DIL_QBLOCK, DIL_KBLOCK), F32)],
        compiler_params=pltpu.CompilerParams(
            dimension_semantics=("arbitrary", "arbitrary"), vmem_limit_bytes=VMEM_LIMIT),
        name="dil_attn",
    )(d1, d1, d4, d4, d4, d16, d16, d16, bases)


def _by_pair(bases):
    t = bases.reshape((N_PAIRS, HEADS_PER_TILE) + bases.shape[1:])
    return jnp.moveaxis(t, 1, -2)


def _t5_bucket(rel):
    nb = T5_BUCKETS // 2
    max_exact = nb // 2
    n = jnp.abs(rel)
    large = max_exact + (jnp.log(jnp.maximum(n, 1).astype(F32) / max_exact)
                         / math.log(T5_MAX_DIST / max_exact) * (nb - max_exact)).astype(jnp.int32)
    large = jnp.minimum(large, nb - 1)
    return jnp.where(rel > 0, nb, 0) + jnp.where(n < max_exact, n, large)


def _na_bases(rel_bias):
    rb = rel_bias.astype(F32)
    gap = jnp.full(rb.shape[:1] + (2 * NA_KH - 2, GRID_W - 2 * NA_KW + 1), -jnp.inf, F32)
    bases = jnp.concatenate(
        [rb[:, :-1, NA_KW - 1:], gap, rb[:, 1:, :], gap, rb[:, :-1, :NA_KW - 1]], axis=-1)
    return _by_pair(bases)


def _dil_bases(t5_bias):
    span = 2 * DIL_RADIUS + 1
    off = np.arange(span) - DIL_RADIUS
    dils = np.asarray([d for _, d in DIL_PATTERNS])
    bucket = _t5_bucket(jnp.asarray(off[None, :] * dils[:, None], jnp.int32))
    vec = jnp.take(t5_bias.astype(F32), bucket, axis=1)
    width = 2 * DIL_KBLOCK
    per_var = []
    for v in range(DIL_VARIANTS):
        lo = DIL_RADIUS * (1 - v)
        pos = jnp.pad(vec[..., max(lo, 0):], ((0, 0), (0, 0), (max(-lo, 0), 0)),
                      constant_values=-jnp.inf)
        pos = jnp.pad(pos, ((0, 0), (0, 0), (0, width - pos.shape[-1] - max(lo, 0))),
                      constant_values=-jnp.inf)
        per_var.append(jnp.concatenate([pos, vec[..., :max(lo, 0)]], axis=-1))
    return _by_pair(jnp.stack(per_var, axis=2))


def _cast_kernel(*refs):
    n = len(refs) // 2
    for src, dst in zip(refs[:n], refs[n:]):
        dst[...] = src[...].astype(BF16)


def _cast_bf16(arrays, *, n_steps=8):
    specs = [pl.BlockSpec((a.shape[0] // n_steps, a.shape[1]), lambda i: (i, 0)) for a in arrays]
    return pl.pallas_call(
        _cast_kernel,
        grid=(n_steps,),
        in_specs=specs,
        out_specs=specs,
        out_shape=[jax.ShapeDtypeStruct(a.shape, BF16) for a in arrays],
        compiler_params=pltpu.CompilerParams(
            dimension_semantics=("arbitrary",), vmem_limit_bytes=VMEM_LIMIT),
        name="cast_weights",
    )(*arrays)


def _group_columns_kernel(w_ref, o_ref):
    is_q = pl.program_id(0) % 3 == 0
    o_ref[...] = (w_ref[...] * jnp.where(is_q, HEAD_DIM ** -0.5, 1.0)).astype(BF16)


def _group_columns(w_qkv):
    return pl.pallas_call(
        _group_columns_kernel,
        grid=(2 * 3,),
        in_specs=[pl.BlockSpec((D_MODEL, D_NA), lambda j: (0, (j % 3) * 2 + j // 3))],
        out_specs=pl.BlockSpec((D_MODEL, D_NA), lambda j: (0, j)),
        out_shape=jax.ShapeDtypeStruct((D_MODEL, 2 * D_GROUP), BF16),
        compiler_params=pltpu.CompilerParams(
            dimension_semantics=("arbitrary",), vmem_limit_bytes=VMEM_LIMIT),
        name="group_qkv_columns",
    )(w_qkv)


def kernel(x, ffn1_pre_g, ffn1_w_gate, ffn1_w_up, ffn1_w_down, ffn1_post_g, mix_pre_g, w_qkv,
           na_rel_bias, t5_rel_bias, na_out_g, dil_out_g, w_out, mix_post_g, ffn2_pre_g,
           ffn2_w_gate, ffn2_w_up, ffn2_w_down, ffn2_post_g):
    b, s, d = x.shape
    depth = ffn1_pre_g.shape[0]
    dil_bases = _dil_bases(t5_rel_bias)
    for l in range(depth):
        wg1, wu1, wd1, wg2, wu2, wd2, wo = _cast_bf16(
            [ffn1_w_gate[l], ffn1_w_up[l], ffn1_w_down[l],
             ffn2_w_gate[l], ffn2_w_up[l], ffn2_w_down[l], w_out[l]])
        x = _ffn(x, (ffn1_pre_g[l][None], wg1, wu1, wd1, ffn1_post_g[l][None]))
        qkv_na, d1, d4, d16 = _qkv(x, mix_pre_g[l][None], _group_columns(w_qkv[l]))
        o_na = _na(qkv_na, _na_bases(na_rel_bias[l]))
        o_dil = _dil(d1, d4, d16, dil_bases)
        x = _mix_ffn(x, o_na, o_dil, na_out_g[l][None], dil_out_g[l][None], wo,
                     mix_post_g[l][None],
                     (ffn2_pre_g[l][None], wg2, wu2, wd2, ffn2_post_g[l][None]))
    return x
```

```python
import functools
import math

import jax
import jax.numpy as jnp
import numpy as np
from jax import lax
from jax.experimental import pallas as pl
from jax.experimental.pallas import tpu as pltpu

D_MODEL = 1024
HEAD_DIM = 64
N_HEADS_NA = 8
N_HEADS_DIL = 8
D_NA = N_HEADS_NA * HEAD_DIM
D_DIL = N_HEADS_DIL * HEAD_DIM
D_FF = 2816
GRID_W = 64
NA_KH = 8
NA_KW = 16
DIL_PATTERNS = ((128, 1), (512, 4), (2048, 16))
DIL_QBLOCK = 128
DIL_RADIUS = 64
DIL_KBLOCK = DIL_QBLOCK + 2 * DIL_RADIUS
DIL_VARIANTS = 3
DIL_MID, DIL_FAR = DIL_PATTERNS[1][1], DIL_PATTERNS[2][1]
assert DIL_PATTERNS[0][1] == 1 and DIL_FAR == DIL_MID ** 2
assert all(w // (2 * d) == DIL_RADIUS for w, d in DIL_PATTERNS)
T5_BUCKETS = 32
T5_MAX_DIST = 1024
NORM_EPS = 1e-6

LANES = 128
MXU_COLS = 256
NA_GROUP, DIL_GROUP = 16, 4
FFN_ROWS = 128
HEADS_PER_TILE = LANES // HEAD_DIM
N_PAIRS = N_HEADS_NA // HEADS_PER_TILE
D_GROUP = 3 * D_NA
VMEM_LIMIT = 56 * 1024 * 1024

BF16 = jnp.bfloat16
F32 = jnp.float32


def _rms(x, g):
    return x * lax.rsqrt(jnp.mean(x * x, axis=-1, keepdims=True) + NORM_EPS) * g


def _resident(shape):
    return pl.BlockSpec(shape, lambda *_: (0,) * len(shape), pipeline_mode=pl.Buffered(1))


def _half_step_ffn(xs, pre_g_ref, wg_ref, wu_ref, wd_ref, post_g_ref):
    hs = [_rms(x, pre_g_ref[...]).astype(BF16) for x in xs]
    gus = [(jnp.dot(h, wg_ref[...], preferred_element_type=F32),
            jnp.dot(h, wu_ref[...], preferred_element_type=F32)) for h in hs]
    acts = [(g * jax.nn.sigmoid(g) * u).astype(BF16) for g, u in gus]
    ys = [jnp.dot(a, wd_ref[...], preferred_element_type=F32) for a in acts]
    return [x + 0.5 * _rms(y, post_g_ref[...]) for x, y in zip(xs, ys)]


def _row_tiles(n_rows):
    return [slice(r, r + FFN_ROWS) for r in range(0, n_rows, FFN_ROWS)]


def _ffn_kernel(x_ref, *refs):
    o_ref = refs[-1]
    tiles = _row_tiles(x_ref.shape[0])
    for t, out in zip(tiles, _half_step_ffn([x_ref[t, :] for t in tiles], *refs[:-1])):
        o_ref[t, :] = out


def _mix_ffn_kernel(x_ref, na_ref, dil_ref, g_na_ref, g_dil_ref, w_ref, g_mix_ref, *refs):
    o_ref = refs[-1]
    tiles = _row_tiles(x_ref.shape[0])
    heads = lambda ref, t: jnp.concatenate([ref[p, t, :] for p in range(N_PAIRS)], axis=-1)
    xs = []
    for t in tiles:
        a = _rms(heads(na_ref, t), g_na_ref[...]).astype(BF16)
        d = _rms(heads(dil_ref, t), g_dil_ref[...]).astype(BF16)
        mixed = (jnp.dot(a, w_ref[:D_NA, :], preferred_element_type=F32)
                 + jnp.dot(d, w_ref[D_NA:, :], preferred_element_type=F32))
        xs.append(x_ref[t, :] + _rms(mixed, g_mix_ref[...]))
    for t, out in zip(tiles, _half_step_ffn(xs, *refs[:-1])):
        o_ref[t, :] = out


def _ffn_specs():
    return [_resident((1, D_MODEL)), _resident((D_MODEL, D_FF)), _resident((D_MODEL, D_FF)),
            _resident((D_FF, D_MODEL)), _resident((1, D_MODEL))]


def _ffn(x, ffn_params, *, tm=1024):
    b, s, _ = x.shape
    row = pl.BlockSpec((None, tm, D_MODEL), lambda bi, i: (bi, i, 0))
    return pl.pallas_call(
        _ffn_kernel,
        grid=(b, s // tm),
        in_specs=[row] + _ffn_specs(),
        out_specs=row,
        out_shape=jax.ShapeDtypeStruct((b, s, D_MODEL), F32),
        compiler_params=pltpu.CompilerParams(
            dimension_semantics=("arbitrary", "arbitrary"), vmem_limit_bytes=VMEM_LIMIT),
        name="ffn",
    )(x, *ffn_params)


def _mix_ffn(x, o_na, o_dil, g_na, g_dil, w, g_mix, ffn_params, *, tm=512):
    b, s, _ = x.shape
    row = pl.BlockSpec((None, tm, D_MODEL), lambda bi, i: (bi, i, 0))
    heads = pl.BlockSpec((None, N_PAIRS, tm, LANES), lambda bi, i: (bi, 0, i, 0))
    return pl.pallas_call(
        _mix_ffn_kernel,
        grid=(b, s // tm),
        in_specs=[row, heads, heads, _resident((1, D_NA)), _resident((1, D_DIL)),
                  _resident((D_NA + D_DIL, D_MODEL)), _resident((1, D_MODEL))] + _ffn_specs(),
        out_specs=row,
        out_shape=jax.ShapeDtypeStruct((b, s, D_MODEL), F32),
        compiler_params=pltpu.CompilerParams(
            dimension_semantics=("arbitrary", "arbitrary"), vmem_limit_bytes=VMEM_LIMIT),
        name="mix_ffn",
    )(x, o_na, o_dil, g_na, g_dil, w, g_mix, *ffn_params)


def _qkv_kernel(x_ref, g_ref, w_ref, na_ref, d1_ref, d4_ref, d16_ref, scr_ref, scr4_ref, *, tm):
    h = _rms(x_ref[0], g_ref[...]).astype(BF16)
    for c in range(0, D_GROUP, MXU_COLS):
        y = jnp.dot(h, w_ref[:, D_GROUP + c:D_GROUP + c + MXU_COLS], preferred_element_type=F32)
        for lo in range(0, MXU_COLS, LANES):
            t = (c + lo) // LANES
            y_t = y[:, lo:lo + LANES]
            if t >= N_PAIRS:
                d1_ref[0, t - N_PAIRS] = y_t.astype(BF16)
            scr_ref[t] = y_t
            for r in range(DIL_MID):
                cls = scr_ref[t, pl.ds(r, tm // DIL_MID, stride=DIL_MID), :]
                d4_ref[0, t, r] = cls.astype(BF16)
                scr4_ref[t, r] = cls
                for r2 in range(DIL_MID):
                    d16_ref[0, t, DIL_MID * r2 + r] = (
                        scr4_ref[t, r, pl.ds(r2, tm // DIL_FAR, stride=DIL_MID), :].astype(BF16))
        y = jnp.dot(h, w_ref[:, c:c + MXU_COLS], preferred_element_type=F32)
        for lo in range(0, MXU_COLS, LANES):
            na_ref[0, (c + lo) // LANES] = y[:, lo:lo + LANES].astype(BF16)


def _qkv(x, g, w, *, tm=512):
    b, s, _ = x.shape
    n_t = D_GROUP // LANES
    return pl.pallas_call(
        functools.partial(_qkv_kernel, tm=tm),
        grid=(b, s // tm),
        in_specs=[
            pl.BlockSpec((1, tm, D_MODEL), lambda bi, i: (bi, i, 0)),
            _resident((1, D_MODEL)),
            _resident((D_MODEL, 2 * D_GROUP)),
        ],
        out_specs=[
            pl.BlockSpec((1, n_t, tm, LANES), lambda bi, i: (bi, 0, i, 0)),
            pl.BlockSpec((1, n_t - N_PAIRS, tm, LANES), lambda bi, i: (bi, 0, i, 0)),
            pl.BlockSpec((1, n_t, DIL_MID, tm // DIL_MID, LANES), lambda bi, i: (bi, 0, 0, i, 0)),
            pl.BlockSpec((1, n_t, DIL_FAR, tm // DIL_FAR, LANES), lambda bi, i: (bi, 0, 0, i, 0)),
        ],
        out_shape=[
            jax.ShapeDtypeStruct((b, n_t, s, LANES), BF16),
            jax.ShapeDtypeStruct((b, n_t - N_PAIRS, s, LANES), BF16),
            jax.ShapeDtypeStruct((b, n_t, DIL_MID, s // DIL_MID, LANES), BF16),
            jax.ShapeDtypeStruct((b, n_t, DIL_FAR, s // DIL_FAR, LANES), BF16),
        ],
        scratch_shapes=[pltpu.VMEM((n_t, tm, LANES), F32),
                        pltpu.VMEM((n_t, DIL_MID, tm // DIL_MID, LANES), F32)],
        compiler_params=pltpu.CompilerParams(
            dimension_semantics=("arbitrary", "arbitrary"), vmem_limit_bytes=VMEM_LIMIT),
        name="qkv",
    )(x, g, w)


def _second_head():
    return lax.broadcasted_iota(jnp.int32, (1, LANES), 1) >= HEAD_DIM


def _scores(blocks):
    second = _second_head()
    out = []
    for q, k, _, table in blocks:
        zero = jnp.zeros_like(q)
        q2 = jnp.concatenate([jnp.where(second, zero, q), jnp.where(second, q, zero)], axis=0)
        out.append(lax.dot_general(q2, k, (((1,), (1,)), ((), ())),
                                   preferred_element_type=F32) + table)
    return out


def _softmaxes(scores):
    out = []
    for s in scores:
        m = jnp.max(s, axis=-1, keepdims=True)
        p = jnp.exp(s - m)
        out.append((m, jnp.sum(p, axis=-1, keepdims=True), p.astype(BF16)))
    return out


def _weighted_values(blocks, probs):
    second = _second_head()
    out = []
    for (q, _, v, _), (m, l, p) in zip(blocks, probs):
        m_rows = q.shape[0]
        pv = jnp.dot(p, v, preferred_element_type=F32)
        pick = lambda a, m_rows=m_rows: jnp.where(second, a[m_rows:], a[:m_rows])
        out.append((pick(m), pick(l), pick(pv)))
    return out


def _attend_blocks(blocks, group):
    for lo in range(0, len(blocks), group):
        blks = blocks[lo:lo + group]
        yield from _weighted_values(blks, _softmaxes(_scores(blks)))


def _toeplitz_rows(base_row, n_rows, first=0, step=1):
    base = jnp.broadcast_to(base_row, (n_rows, base_row.shape[-1]))
    return pltpu.roll(base, first, 1, stride=step, stride_axis=0)


def _na_kernel(q_ref, k_ref, v_ref, base_ref, o_ref, tbl_ref):
    rows = q_ref.shape[0] // GRID_W
    n_keys = NA_KH * GRID_W

    @pl.when(pl.program_id(1) == 0)
    def _():
        qc = lax.broadcasted_iota(jnp.int32, (GRID_W, LANES), 0)
        kc = lax.broadcasted_iota(jnp.int32, (GRID_W, LANES), 1) % GRID_W
        qs = jnp.clip(qc - NA_KW // 2, 0, GRID_W - NA_KW)
        valid = (kc >= qs) & (kc < qs + NA_KW)
        for h in range(HEADS_PER_TILE):
            for dr in range(2 * NA_KH - 2):
                tile = jnp.where(valid, _toeplitz_rows(base_ref[dr, h:h + 1, :], GRID_W), -jnp.inf)
                for t in range(n_keys // LANES):
                    v = NA_KH - 1 - dr + 2 * t
                    if 0 <= v < NA_KH:
                        tbl_ref[v, h * GRID_W:(h + 1) * GRID_W, t * LANES:(t + 1) * LANES] = tile

    blocks = []
    for r in range(rows):
        kr0 = min(max(r - NA_KH // 2, 0), rows - NA_KH)
        keys = pl.ds(kr0 * GRID_W, n_keys)
        blocks.append((q_ref[pl.ds(r * GRID_W, GRID_W), :], k_ref[keys, :], v_ref[keys, :],
                       tbl_ref[r - kr0]))
    for r, (_, l_t, acc_t) in enumerate(_attend_blocks(blocks, NA_GROUP)):
        o_ref[pl.ds(r * GRID_W, GRID_W), :] = acc_t / l_t


def _na(qkv_na, bases):
    b, _, s, _ = qkv_na.shape
    blk = lambda off: pl.BlockSpec((None, None, s, LANES), lambda p, bi: (bi, off + p, 0, 0))
    return pl.pallas_call(
        _na_kernel,
        grid=(N_PAIRS, b),
        in_specs=[
            blk(0), blk(N_PAIRS), blk(2 * N_PAIRS),
            pl.BlockSpec((None,) + bases.shape[1:], lambda p, bi: (p, 0, 0, 0)),
        ],
        out_specs=pl.BlockSpec((None, None, s, LANES), lambda p, bi: (bi, p, 0, 0)),
        out_shape=jax.ShapeDtypeStruct((b, N_PAIRS, s, LANES), F32),
        scratch_shapes=[pltpu.VMEM((NA_KH, HEADS_PER_TILE * GRID_W, NA_KH * GRID_W), F32)],
        compiler_params=pltpu.CompilerParams(
            dimension_semantics=("arbitrary", "arbitrary"), vmem_limit_bytes=VMEM_LIMIT),
        name="na_attn",
    )(qkv_na, qkv_na, qkv_na, bases)


def _dil_kernel(k1_ref, v1_ref, q4_ref, k4_ref, v4_ref, q16_ref, k16_ref, v16_ref,
                base_ref, o_ref, m_ref, l_ref, acc_ref, tbl_ref):
    s_len = k1_ref.shape[0]
    qb = DIL_QBLOCK
    kb = DIL_KBLOCK
    mid = DIL_MID
    sub = qb // mid
    state = (m_ref, l_ref, acc_ref)

    @pl.when(pl.program_id(1) == 0)
    def _():
        for pat in range(len(DIL_PATTERNS)):
            for var in range(DIL_VARIANTS):
                for h in range(HEADS_PER_TILE):
                    base = base_ref[pat, var, h:h + 1, :]
                    if pat == 0:
                        for a in range(mid):
                            lo = h * qb + a * sub
                            tbl_ref[pat, var, lo:lo + sub, :] = (
                                _toeplitz_rows(base, sub, first=a, step=mid)[:, :kb])
                    else:
                        tbl_ref[pat, var, h * qb:(h + 1) * qb, :] = _toeplitz_rows(base, qb)[:, :kb]

    def merge(old, new):
        m_new = jnp.maximum(old[0], new[0])
        a = jnp.exp(old[0] - m_new)
        bb = jnp.exp(new[0] - m_new)
        return m_new, a * old[1] + bb * new[1], a * old[2] + bb * new[2]


    def window(i, n_blocks):
        q0 = i * qb
        w0 = min(max(q0 - DIL_RADIUS, 0), n_blocks * qb - kb)
        var = 0 if i == 0 else (DIL_VARIANTS - 1 if i == n_blocks - 1 else 1)
        return q0, w0, var

    def de_interleaved(pat, dil, q_ref, k_ref, v_ref):
        nb = s_len // dil // qb
        first = dil > mid
        step = dil // mid
        blocks, finishers = [], []
        for idx in range(dil * nb):
            c = idx // nb
            q0, w0, var = window(idx % nb, nb)
            blocks.append((q_ref[c, pl.ds(q0, qb), :], k_ref[c, pl.ds(w0, kb), :],
                           v_ref[c, pl.ds(w0, kb), :], tbl_ref[pat, var]))
            r = c % mid
            rows = pl.ds(c // mid + q0 * step, qb, stride=step) if first else pl.ds(q0, qb)

            def finish(pieces, r=r, rows=rows):
                if not first:
                    pieces = merge([ref[r, rows, :] for ref in state], pieces)
                for ref, piece in zip(state, pieces):
                    ref[r, rows, :] = piece

            finishers.append(finish)
        return blocks, finishers

    def token_order():
        nb1 = s_len // qb
        gather = lambda ref, p0: jnp.concatenate(
            [ref[a, pl.ds(p0, sub), :] for a in range(mid)], axis=0)
        blocks, finishers = [], []
        for i in range(nb1):
            q0, w0, var = window(i, nb1)
            p0 = i * sub
            blocks.append((gather(q4_ref, p0), k1_ref[pl.ds(w0, kb), :],
                           v1_ref[pl.ds(w0, kb), :], tbl_ref[0, var]))

            def finish(pieces, q0=q0, p0=p0):
                _, l_n, acc_n = merge([gather(ref, p0) for ref in state], pieces)
                out = acc_n / l_n
                for a in range(mid):
                    o_ref[pl.ds(q0 + a, sub, stride=mid), :] = out[a * sub:(a + 1) * sub]

            finishers.append(finish)
        return blocks, finishers

    def run(phase):
        blocks, finishers = phase
        for fin, pieces in zip(finishers, _attend_blocks(blocks, DIL_GROUP)):
            fin(pieces)

    run(de_interleaved(2, DIL_FAR, q16_ref, k16_ref, v16_ref))
    run(de_interleaved(1, mid, q4_ref, k4_ref, v4_ref))
    run(token_order())


def _dil(d1, d4, d16, bases):
    b, _, s, _ = d1.shape
    mid = DIL_MID
    blk1 = lambda off: pl.BlockSpec((None, None, s, LANES), lambda p, bi: (bi, off + p, 0, 0))
    blkd = lambda d, off: pl.BlockSpec((None, None, d, s // d, LANES),
                                       lambda p, bi: (bi, off + p, 0, 0, 0))
    specs = [blk1(0), blk1(N_PAIRS)]
    for d in (mid, DIL_FAR):
        specs += [blkd(d, 0), blkd(d, N_PAIRS), blkd(d, 2 * N_PAIRS)]
    specs.append(pl.BlockSpec((None,) + bases.shape[1:], lambda p, bi: (p, 0, 0, 0, 0)))
    n_pat, n_var = bases.shape[1:3]
    return pl.pallas_call(
        _dil_kernel,
        grid=(N_PAIRS, b),
        in_specs=specs,
        out_specs=pl.BlockSpec((None, None, s, LANES), lambda p, bi: (bi, p, 0, 0)),
        out_shape=jax.ShapeDtypeStruct((b, N_PAIRS, s, LANES), F32),
        scratch_shapes=[pltpu.VMEM((mid, s // mid, LANES), F32)] * 3
        + [pltpu.VMEM((n_pat, n_var, HEADS_PER_TILE * DIL_QBLOCK, DIL_KBLOCK), F32)],
        compiler_params=pltpu.CompilerParams(
            dimension_semantics=("arbitrary", "arbitrary"), vmem_limit_bytes=VMEM_LIMIT),
        name="dil_attn",
    )(d1, d1, d4, d4, d4, d16, d16, d16, bases)


def _by_pair(bases):
    t = bases.reshape((N_PAIRS, HEADS_PER_TILE) + bases.shape[1:])
    return jnp.moveaxis(t, 1, -2)


def _t5_bucket(rel):
    nb = T5_BUCKETS // 2
    max_exact = nb // 2
    n = jnp.abs(rel)
    large = max_exact + (jnp.log(jnp.maximum(n, 1).astype(F32) / max_exact)
                         / math.log(T5_MAX_DIST / max_exact) * (nb - max_exact)).astype(jnp.int32)
    large = jnp.minimum(large, nb - 1)
    return jnp.where(rel > 0, nb, 0) + jnp.where(n < max_exact, n, large)


def _na_bases(rel_bias):
    rb = rel_bias.astype(F32)
    gap = jnp.full(rb.shape[:1] + (2 * NA_KH - 2, GRID_W - 2 * NA_KW + 1), -jnp.inf, F32)
    bases = jnp.concatenate(
        [rb[:, :-1, NA_KW - 1:], gap, rb[:, 1:, :], gap, rb[:, :-1, :NA_KW - 1]], axis=-1)
    return _by_pair(bases)


def _dil_bases(t5_bias):
    span = 2 * DIL_RADIUS + 1
    off = np.arange(span) - DIL_RADIUS
    dils = np.asarray([d for _, d in DIL_PATTERNS])
    bucket = _t5_bucket(jnp.asarray(off[None, :] * dils[:, None], jnp.int32))
    vec = jnp.take(t5_bias.astype(F32), bucket, axis=1)
    width = 2 * DIL_KBLOCK
    per_var = []
    for v in range(DIL_VARIANTS):
        lo = DIL_RADIUS * (1 - v)
        pos = jnp.pad(vec[..., max(lo, 0):], ((0, 0), (0, 0), (max(-lo, 0), 0)),
                      constant_values=-jnp.inf)
        pos = jnp.pad(pos, ((0, 0), (0, 0), (0, width - pos.shape[-1] - max(lo, 0))),
                      constant_values=-jnp.inf)
        per_var.append(jnp.concatenate([pos, vec[..., :max(lo, 0)]], axis=-1))
    return _by_pair(jnp.stack(per_var, axis=2))


def _cast_kernel(*refs):
    n = len(refs) // 2
    for src, dst in zip(refs[:n], refs[n:]):
        dst[...] = src[...].astype(BF16)


def _cast_bf16(arrays, *, n_steps=8):
    specs = [pl.BlockSpec((a.shape[0] // n_steps, a.shape[1]), lambda i: (i, 0)) for a in arrays]
    return pl.pallas_call(
        _cast_kernel,
        grid=(n_steps,),
        in_specs=specs,
        out_specs=specs,
        out_shape=[jax.ShapeDtypeStruct(a.shape, BF16) for a in arrays],
        compiler_params=pltpu.CompilerParams(
            dimension_semantics=("arbitrary",), vmem_limit_bytes=VMEM_LIMIT),
        name="cast_weights",
    )(*arrays)


def _group_columns_kernel(w_ref, o_ref):
    is_q = pl.program_id(0) % 3 == 0
    o_ref[...] = (w_ref[...] * jnp.where(is_q, HEAD_DIM ** -0.5, 1.0)).astype(BF16)


def _group_columns(w_qkv):
    return pl.pallas_call(
        _group_columns_kernel,
        grid=(2 * 3,),
        in_specs=[pl.BlockSpec((D_MODEL, D_NA), lambda j: (0, (j % 3) * 2 + j // 3))],
        out_specs=pl.BlockSpec((D_MODEL, D_NA), lambda j: (0, j)),
        out_shape=jax.ShapeDtypeStruct((D_MODEL, 2 * D_GROUP), BF16),
        compiler_params=pltpu.CompilerParams(
            dimension_semantics=("arbitrary",), vmem_limit_bytes=VMEM_LIMIT),
        name="group_qkv_columns",
    )(w_qkv)


def kernel(x, ffn1_pre_g, ffn1_w_gate, ffn1_w_up, ffn1_w_down, ffn1_post_g, mix_pre_g, w_qkv,
           na_rel_bias, t5_rel_bias, na_out_g, dil_out_g, w_out, mix_post_g, ffn2_pre_g,
           ffn2_w_gate, ffn2_w_up, ffn2_w_down, ffn2_post_g):
    b, s, d = x.shape
    depth = ffn1_pre_g.shape[0]
    dil_bases = _dil_bases(t5_rel_bias)
    for l in range(depth):
        wg1, wu1, wd1, wg2, wu2, wd2, wo = _cast_bf16(
            [ffn1_w_gate[l], ffn1_w_up[l], ffn1_w_down[l],
             ffn2_w_gate[l], ffn2_w_up[l], ffn2_w_down[l], w_out[l]])
        x = _ffn(x, (ffn1_pre_g[l][None], wg1, wu1, wd1, ffn1_post_g[l][None]))
        qkv_na, d1, d4, d16 = _qkv(x, mix_pre_g[l][None], _group_columns(w_qkv[l]))
        o_na = _na(qkv_na, _na_bases(na_rel_bias[l]))
        o_dil = _dil(d1, d4, d16, dil_bases)
        x = _mix_ffn(x, o_na, o_dil, na_out_g[l][None], dil_out_g[l][None], wo,
                     mix_post_g[l][None],
                     (ffn2_pre_g[l][None], wg2, wu2, wd2, ffn2_post_g[l][None]))
    return x
```
